```python
import jax, jax.numpy as jnp
from jax import lax
import numpy as np

D_MODEL = 1024
BATCH = 4
SEQ = 4096
DEPTH = 4
DEC_BATCH = 128
DEC_SEQ = 8
PAST_LEN = 8192
PAGE_SIZE = 128

N_MIXERS = 2
N_FOX = (DEPTH + 1) // 2
N_MLA = DEPTH // 2
FOX_HEADS = 16
FOX_KV_HEADS = 4
FOX_HEAD_DIM = D_MODEL // FOX_HEADS
FOX_GROUP = FOX_HEADS // FOX_KV_HEADS
FOX_IN = FOX_HEADS * FOX_HEAD_DIM + 2 * FOX_KV_HEADS * FOX_HEAD_DIM + FOX_HEADS
FOX_SCALE = FOX_HEAD_DIM ** -0.5
MLA_HEADS = 16
MLA_NOPE_DIM = 64
MLA_ROPE_DIM = 32
MLA_V_DIM = 64
MLA_Q_LORA = 384
MLA_KV_LORA = 256
MLA_IN = MLA_Q_LORA + MLA_KV_LORA + MLA_ROPE_DIM
MLA_SCALE = (MLA_NOPE_DIM + MLA_ROPE_DIM) ** -0.5
D_FF = -(-8 * D_MODEL // (3 * 256)) * 256
QBLOCK = 128
ROPE_THETA = 10000.0
EPS = 1e-6

kernel_name = "fox_mla_hybrid_decode_step"


def _rms(x, g):
    xf = x.astype(jnp.float32)
    y = xf * lax.rsqrt(jnp.mean(xf * xf, axis=-1, keepdims=True) + EPS)
    return (y * g.astype(jnp.float32)).astype(x.dtype)


def _rope(x, pos):
    half = x.shape[-1] // 2
    inv = ROPE_THETA ** (-jnp.arange(half, dtype=jnp.float32) / half)
    ang = pos.astype(jnp.float32)[:, None] * inv[None, :]
    ang = ang.reshape((ang.shape[0],) + (1,) * (x.ndim - 3) + (half,))
    cos, sin = jnp.cos(ang), jnp.sin(ang)
    xf = x.astype(jnp.float32)
    x1, x2 = xf[..., :half], xf[..., half:]
    return jnp.concatenate([x1 * cos - x2 * sin, x2 * cos + x1 * sin], axis=-1).astype(x.dtype)


def _to_blocks(x):
    b, s = x.shape[:2]
    return x.reshape((b, s // QBLOCK, QBLOCK) + x.shape[2:]).swapaxes(0, 1)


def _from_blocks(o):
    nb, b, q = o.shape[:3]
    return o.swapaxes(0, 1).reshape((b, nb * q) + o.shape[3:])


def _swiglu(xn, w_gu, w_down):
    g, u = jnp.split(xn @ w_gu, 2, axis=-1)
    return (jax.nn.silu(g) * u) @ w_down


def _fox_project(xn, w_in, b_f, g_q, g_k):
    b, t, _ = xn.shape
    nq = FOX_HEADS * FOX_HEAD_DIM
    nk = FOX_KV_HEADS * FOX_HEAD_DIM
    z = xn @ w_in
    q = _rms(z[..., :nq].reshape(b, t, FOX_HEADS, FOX_HEAD_DIM), g_q)
    k = _rms(z[..., nq:nq + nk].reshape(b, t, FOX_KV_HEADS, FOX_HEAD_DIM), g_k)
    v = z[..., nq + nk:nq + 2 * nk].reshape(b, t, FOX_KV_HEADS, FOX_HEAD_DIM)
    lf = jax.nn.log_sigmoid((z[..., nq + 2 * nk:] + b_f).astype(jnp.float32))
    return q, k, v, lf


def _fox_attend(q, k, v, cq, ck, qpos, kpos):
    b, nq = q.shape[:2]
    ns = k.shape[1]
    qg = q.reshape(b, nq, FOX_KV_HEADS, FOX_GROUP, FOX_HEAD_DIM)
    s = jnp.einsum('bqkgd,bskd->bkgqs', qg, k).astype(jnp.float32) * FOX_SCALE
    bias = (jnp.swapaxes(cq, 1, 2)[..., :, None] - jnp.swapaxes(ck, 1, 2)[..., None, :])
    bias = bias.reshape(b, FOX_KV_HEADS, FOX_GROUP, nq, ns)
    mask = kpos[None, :] <= qpos[:, None]
    p = jax.nn.softmax(jnp.where(mask, s + bias, -jnp.inf), axis=-1)
    o = jnp.einsum('bkgqs,bskd->bqkgd', p.astype(v.dtype), v)
    return o.reshape(b, nq, FOX_HEADS * FOX_HEAD_DIM)


def _fox_prompt(q, k, v, lf):
    s = q.shape[1]
    c = jnp.cumsum(lf, axis=1)
    pos = jnp.arange(s, dtype=jnp.int32)

    def block(args):
        q_b, c_b, pos_b = args
        return _fox_attend(q_b, k, v, c_b, c, pos_b, pos)

    return _from_blocks(lax.map(block, (_to_blocks(q), _to_blocks(c), pos.reshape(-1, QBLOCK))))


def _gather_rows(cache, page_table):
    g = cache[page_table]
    return g.reshape((g.shape[0], g.shape[1] * g.shape[2]) + g.shape[3:])


def _mla_project(xn, pos, w_in, g_cq, w_uq, g_qn, g_qr, g_ckv, g_kr):
    z = xn @ w_in
    cq = _rms(z[..., :MLA_Q_LORA], g_cq)
    ckv = _rms(z[..., MLA_Q_LORA:MLA_Q_LORA + MLA_KV_LORA], g_ckv)
    kpe = _rope(_rms(z[..., MLA_Q_LORA + MLA_KV_LORA:], g_kr), pos)
    q = jnp.einsum('btc,chd->bthd', cq, w_uq)
    qn = _rms(q[..., :MLA_NOPE_DIM], g_qn)
    qr = _rope(_rms(q[..., MLA_NOPE_DIM:], g_qr), pos)
    return qn, qr, ckv, kpe


def _mla_scores(qn, qr, kn, kpe):
    s = jnp.einsum('bqhd,bshd->bhqs', qn, kn) + jnp.einsum('bqhr,bsr->bhqs', qr, kpe)
    return s.astype(jnp.float32) * MLA_SCALE


def _mla_key_nope(ckv, w_uk, g_kn):
    return _rms(jnp.einsum('bsc,chd->bshd', ckv, w_uk), g_kn)


def _mla_prompt(qn, qr, ckv, kpe, w_ukv, g_kn):
    s = qn.shape[1]
    kn = _mla_key_nope(ckv, w_ukv[..., :MLA_NOPE_DIM], g_kn)
    v = jnp.einsum('bsc,chd->bshd', ckv, w_ukv[..., MLA_NOPE_DIM:])
    pos = jnp.arange(s, dtype=jnp.int32)

    def block(args):
        qn_b, qr_b, pos_b = args
        logits = _mla_scores(qn_b, qr_b, kn, kpe)
        p = jax.nn.softmax(jnp.where(pos[None, :] <= pos_b[:, None], logits, -jnp.inf), axis=-1)
        o = jnp.einsum('bhqs,bshd->bqhd', p.astype(v.dtype), v)
        return o.reshape(o.shape[0], o.shape[1], MLA_HEADS * MLA_V_DIM)

    return _from_blocks(lax.map(block, (_to_blocks(qn), _to_blocks(qr), pos.reshape(-1, QBLOCK))))


def _mla_sample(qn, qr, ckv_new, kpe_new, ckv_pages, kpe_pages, w_ukv, g_kn):
    w_uk = w_ukv[..., :MLA_NOPE_DIM]
    w_uv = w_ukv[..., MLA_NOPE_DIM:]

    def page_logits(args):
        c_pg, r_pg = args
        return _mla_scores(qn, qr, _mla_key_nope(c_pg, w_uk, g_kn), r_pg)

    lp = lax.map(page_logits, (ckv_pages, kpe_pages))
    n_pg, b, h, t, pg = lp.shape
    lp = lp.transpose(1, 2, 3, 0, 4).reshape(b, h, t, n_pg * pg)
    ln = _mla_scores(qn, qr, _mla_key_nope(ckv_new, w_uk, g_kn), kpe_new)
    idx = jnp.arange(t)
    ln = jnp.where(idx[None, :] <= idx[:, None], ln, -jnp.inf)
    p = jax.nn.softmax(jnp.concatenate([lp, ln], axis=-1), axis=-1)
    ckv_all = jnp.concatenate(
        [ckv_pages.swapaxes(0, 1).reshape(b, n_pg * pg, MLA_KV_LORA), ckv_new.astype(ckv_pages.dtype)], axis=1)
    o_lat = jnp.einsum('bhqs,bsc->bqhc', p.astype(ckv_all.dtype), ckv_all)
    o = jnp.einsum('bqhc,chd->bqhd', o_lat, w_uv)
    return o.reshape(b, t, MLA_HEADS * MLA_V_DIM)


def setup_inputs(seed: int = 0) -> dict:
    key = jax.random.key(seed)
    keys = iter(jax.random.split(key, 40))
    n_pages = PAST_LEN // PAGE_SIZE
    n_used = DEC_BATCH * n_pages
    n_pool = n_used + n_used // 4

    def nrm(shape, scale=1.0):
        return jax.random.normal(next(keys), shape, jnp.float32) * scale

    def gain(shape):
        return 1.0 + 0.05 * jax.random.normal(next(keys), shape, jnp.float32)

    x_prompt = nrm((BATCH, SEQ, D_MODEL))
    x_sample = nrm((DEC_BATCH, DEC_SEQ, D_MODEL))
    cache_fox_k = nrm((N_FOX, n_pool, PAGE_SIZE, FOX_KV_HEADS, FOX_HEAD_DIM))
    cache_fox_v = nrm((N_FOX, n_pool, PAGE_SIZE, FOX_KV_HEADS, FOX_HEAD_DIM))
    cache_fox_logf = jax.nn.log_sigmoid(3.5 + nrm((N_FOX, n_pool, PAGE_SIZE, FOX_HEADS)))
    cache_mla_ckv = nrm((N_MLA, n_pool, PAGE_SIZE, MLA_KV_LORA))
    cache_mla_kpe = nrm((N_MLA, n_pool, PAGE_SIZE, MLA_ROPE_DIM))
    page_table = jax.random.permutation(next(keys), n_pool)[:n_used].reshape(DEC_BATCH, n_pages).astype(jnp.int32)

    attn_norm = gain((DEPTH, D_MODEL))
    ffn_norm = gain((DEPTH, D_MODEL))
    fox_w_in = nrm((N_FOX, D_MODEL, FOX_IN), D_MODEL ** -0.5)
    fox_b_f = jax.random.uniform(next(keys), (N_FOX, FOX_HEADS), jnp.float32, minval=1.0, maxval=6.0)
    fox_g_q = gain((N_FOX, FOX_HEAD_DIM))
    fox_g_k = gain((N_FOX, FOX_HEAD_DIM))
    fox_w_o = nrm((N_FOX, FOX_HEADS * FOX_HEAD_DIM, D_MODEL), (FOX_HEADS * FOX_HEAD_DIM) ** -0.5)
    mla_w_in = nrm((N_MLA, D_MODEL, MLA_IN), D_MODEL ** -0.5)
    mla_g_cq = gain((N_MLA, MLA_Q_LORA))
    mla_w_uq = nrm((N_MLA, MLA_Q_LORA, MLA_HEADS, MLA_NOPE_DIM + MLA_ROPE_DIM), MLA_Q_LORA ** -0.5)
    mla_g_qn = gain((N_MLA, MLA_NOPE_DIM))
    mla_g_qr = gain((N_MLA, MLA_ROPE_DIM))
    mla_g_ckv = gain((N_MLA, MLA_KV_LORA))
    mla_g_kr = gain((N_MLA, MLA_ROPE_DIM))
    mla_w_ukv = nrm((N_MLA, MLA_KV_LORA, MLA_HEADS, MLA_NOPE_DIM + MLA_V_DIM), MLA_KV_LORA ** -0.5)
    mla_g_kn = gain((N_MLA, MLA_NOPE_DIM))
    mla_w_o = nrm((N_MLA, MLA_HEADS * MLA_V_DIM, D_MODEL), (MLA_HEADS * MLA_V_DIM) ** -0.5)
    ffn_w_gu = nrm((DEPTH, D_MODEL, 2 * D_FF), D_MODEL ** -0.5)
    ffn_w_down = nrm((DEPTH, D_FF, D_MODEL), D_FF ** -0.5)
    return {
        "x_prompt": x_prompt, "x_sample": x_sample,
        "cache_fox_k": cache_fox_k, "cache_fox_v": cache_fox_v, "cache_fox_logf": cache_fox_logf,
        "cache_mla_ckv": cache_mla_ckv, "cache_mla_kpe": cache_mla_kpe, "page_table": page_table,
        "attn_norm": attn_norm, "ffn_norm": ffn_norm,
        "fox_w_in": fox_w_in, "fox_b_f": fox_b_f, "fox_g_q": fox_g_q, "fox_g_k": fox_g_k, "fox_w_o": fox_w_o,
        "mla_w_in": mla_w_in, "mla_g_cq": mla_g_cq, "mla_w_uq": mla_w_uq, "mla_g_qn": mla_g_qn,
        "mla_g_qr": mla_g_qr, "mla_g_ckv": mla_g_ckv, "mla_g_kr": mla_g_kr, "mla_w_ukv": mla_w_ukv,
        "mla_g_kn": mla_g_kn, "mla_w_o": mla_w_o,
        "ffn_w_gu": ffn_w_gu, "ffn_w_down": ffn_w_down,
    }


def reference(x_prompt, x_sample, cache_fox_k, cache_fox_v, cache_fox_logf, cache_mla_ckv, cache_mla_kpe,
              page_table, attn_norm, ffn_norm, fox_w_in, fox_b_f, fox_g_q, fox_g_k, fox_w_o,
              mla_w_in, mla_g_cq, mla_w_uq, mla_g_qn, mla_g_qr, mla_g_ckv, mla_g_kr, mla_w_ukv, mla_g_kn,
              mla_w_o, ffn_w_gu, ffn_w_down):
    s_len = x_prompt.shape[1]
    t_len = x_sample.shape[1]
    past = page_table.shape[1] * cache_fox_k.shape[2]
    pos_p = jnp.arange(s_len, dtype=jnp.int32)
    pos_s = past + jnp.arange(t_len, dtype=jnp.int32)
    kpos_s = jnp.arange(past + t_len, dtype=jnp.int32)
    page_table_t = page_table.T

    h_p, h_s = x_prompt, x_sample
    fk_p, fv_p, fl_p, fk_s, fv_s, fl_s = [], [], [], [], [], []
    mc_p, mr_p, mc_s, mr_s = [], [], [], []
    for i in range(DEPTH):
        j = i // N_MIXERS
        xn_p = _rms(h_p, attn_norm[i])
        xn_s = _rms(h_s, attn_norm[i])
        if i % N_MIXERS == 0:
            q, k, v, lf = _fox_project(xn_p, fox_w_in[j], fox_b_f[j], fox_g_q[j], fox_g_k[j])
            o_p = _fox_prompt(q, k, v, lf)
            q2, k2, v2, lf2 = _fox_project(xn_s, fox_w_in[j], fox_b_f[j], fox_g_q[j], fox_g_k[j])
            k_all = jnp.concatenate([_gather_rows(cache_fox_k[j], page_table), k2.astype(cache_fox_k.dtype)], axis=1)
            v_all = jnp.concatenate([_gather_rows(cache_fox_v[j], page_table), v2.astype(cache_fox_v.dtype)], axis=1)
            lf_all = jnp.concatenate([_gather_rows(cache_fox_logf[j], page_table).astype(jnp.float32), lf2], axis=1)
            c = jnp.cumsum(lf_all, axis=1)
            o_s = _fox_attend(q2, k_all, v_all, c[:, past:], c, pos_s, kpos_s)
            h_p = h_p + o_p @ fox_w_o[j]
            h_s = h_s + o_s @ fox_w_o[j]
            fk_p.append(k); fv_p.append(v); fl_p.append(lf)
            fk_s.append(k2); fv_s.append(v2); fl_s.append(lf2)
        else:
            mla_w = (mla_w_in[j], mla_g_cq[j], mla_w_uq[j], mla_g_qn[j], mla_g_qr[j], mla_g_ckv[j], mla_g_kr[j])
            qn, qr, ckv, kpe = _mla_project(xn_p, pos_p, *mla_w)
            o_p = _mla_prompt(qn, qr, ckv, kpe, mla_w_ukv[j], mla_g_kn[j])
            qn2, qr2, ckv2, kpe2 = _mla_project(xn_s, pos_s, *mla_w)
            o_s = _mla_sample(qn2, qr2, ckv2, kpe2, cache_mla_ckv[j][page_table_t], cache_mla_kpe[j][page_table_t],
                              mla_w_ukv[j], mla_g_kn[j])
            h_p = h_p + o_p @ mla_w_o[j]
            h_s = h_s + o_s @ mla_w_o[j]
            mc_p.append(ckv); mr_p.append(kpe)
            mc_s.append(ckv2); mr_s.append(kpe2)
        h_p = h_p + _swiglu(_rms(h_p, ffn_norm[i]), ffn_w_gu[i], ffn_w_down[i])
        h_s = h_s + _swiglu(_rms(h_s, ffn_norm[i]), ffn_w_gu[i], ffn_w_down[i])

    return (h_p, h_s,
            jnp.stack(fk_p), jnp.stack(fv_p), jnp.stack(fl_p), jnp.stack(mc_p), jnp.stack(mr_p),
            jnp.stack(fk_s), jnp.stack(fv_s), jnp.stack(fl_s), jnp.stack(mc_s), jnp.stack(mr_s))
```

```python
import functools

import numpy as np
import jax
import jax.numpy as jnp
from jax import lax
from jax.experimental import pallas as pl
from jax.experimental.pallas import tpu as pltpu

BF = jnp.bfloat16
F32 = jnp.float32
EPS = 1e-6
ROPE_THETA = 10000.0
LANES = 128
SUBLANES = 8
HEAD_PAD = LANES
NEG = -1e30
VMEM_LIMIT = 56 * 1024 * 1024
NT = (((1,), (1,)), ((), ()))


def _dot(a, b):
    return jnp.dot(a, b, preferred_element_type=F32)


def _dot_nt(a, b):
    return lax.dot_general(a, b, NT, preferred_element_type=F32)


def _split3(x):
    hi = x.astype(BF)
    r1 = x - hi.astype(F32)
    mid = r1.astype(BF)
    lo = (r1 - mid.astype(F32)).astype(BF)
    return hi, mid, lo


def _dot_exact_lhs(x, w):
    hi, mid, lo = _split3(x)
    return _dot(hi, w) + _dot(mid, w) + _dot(lo, w)


def _dot_exact_rhs(w, x):
    hi, mid, lo = _split3(x)
    return _dot(w, hi) + _dot(w, mid) + _dot(w, lo)


def _rms_rows(x, g):
    return x * lax.rsqrt(jnp.mean(x * x, axis=-1, keepdims=True) + EPS) * g


def _log_sigmoid(x):
    return jnp.minimum(x, 0.0) - jnp.log1p(jnp.exp(-jnp.abs(x)))


def _full(shape):
    n = len(shape)
    return pl.BlockSpec(shape, lambda *_: (0,) * n)


def _params(sem):
    return pltpu.CompilerParams(dimension_semantics=sem, vmem_limit_bytes=VMEM_LIMIT)


def _fox_in_kernel(x_ref, ga_ref, wq_ref, wk_ref, wv_ref, wkv_ref, wf_ref, bf_ref, gq_ref, gk_ref, gk2_ref,
                   g1_ref, g2_ref, tri_ref, pq_ref, pk_ref, oq_ref, ok_ref,
                   q_out, ka_out, vp_out, k_out, v_out, lf_out, c_out, carry_ref,
                   *, tiles_per_seq, n_heads, n_kv, n_gate):
    i = pl.program_id(0)
    tm = x_ref.shape[0]
    xn = _rms_rows(x_ref[...], ga_ref[...]).astype(BF)
    lane = lax.broadcasted_iota(jnp.int32, (tm, LANES), 1)

    lf = jnp.where(lane < n_gate, _log_sigmoid(_dot(xn, wf_ref[...]) + bf_ref[...]), 0.0)
    lf_out[...] = lf[:, :n_gate]

    @pl.when(i % tiles_per_seq == 0)
    def _():
        carry_ref[...] = jnp.zeros_like(carry_ref)

    c = _dot_exact_rhs(tri_ref[...], lf) + carry_ref[...]
    carry_ref[...] = c[tm - 1:tm, :]
    c_out[...] = c

    ch, cm, cl = _split3(c)
    cparts = (ch.astype(F32) + pltpu.roll(cm.astype(F32), 16, 1) + pltpu.roll(cl.astype(F32), 32, 1)).astype(BF)

    g1 = g1_ref[...]
    for j in range(n_heads // 2):
        sl = slice(2 * HEAD_PAD * j, 2 * HEAD_PAD * (j + 1))
        zq = _dot(xn, wq_ref[:, sl])
        aug = _dot(cparts, pq_ref[:, sl]) + oq_ref[:, sl]
        for u in range(2):
            z = zq[:, HEAD_PAD * u:HEAD_PAD * (u + 1)]
            ms = _dot((z * z).astype(BF), g1)
            qn = z * lax.rsqrt(ms + EPS) * gq_ref[...]
            q_out[2 * j + u] = (qn + aug[:, HEAD_PAD * u:HEAD_PAD * (u + 1)]).astype(BF)

    for j in range(n_kv // 2):
        sl = slice(2 * HEAD_PAD * j, 2 * HEAD_PAD * (j + 1))
        zk = _dot(xn, wk_ref[:, sl])
        zv = _dot(xn, wv_ref[:, sl])
        aug = _dot(cparts, pk_ref[:, sl]) + ok_ref[:, sl]
        for u in range(2):
            z = zk[:, HEAD_PAD * u:HEAD_PAD * (u + 1)]
            ms = _dot((z * z).astype(BF), g1)
            kn = z * lax.rsqrt(ms + EPS) * gk_ref[...]
            ka_out[2 * j + u] = (kn + aug[:, HEAD_PAD * u:HEAD_PAD * (u + 1)]).astype(BF)
            vp_out[2 * j + u] = zv[:, HEAD_PAD * u:HEAD_PAD * (u + 1)].astype(BF)

    zkv = _dot(xn, wkv_ref[...])
    nk = wkv_ref.shape[1] // 2
    k = zkv[:, :nk]
    ms = _dot((k * k).astype(BF), g2_ref[...])
    k_out[...] = k * lax.rsqrt(ms + EPS) * gk2_ref[...]
    v_out[...] = zkv[:, nk:]


def _fox_in(x, w, *, tm, tiles_per_seq, tri):
    n, d = x.shape
    n_heads, n_kv, n_gate, hd = w["n_heads"], w["n_kv"], w["n_gate"], w["head_dim"]
    kern = functools.partial(_fox_in_kernel, tiles_per_seq=tiles_per_seq, n_heads=n_heads, n_kv=n_kv, n_gate=n_gate)
    consts = [w["ga"], w["wq"], w["wk"], w["wv"], w["wkv"], w["wf"], w["bf"], w["gq"], w["gk"], w["gk2"],
              w["g1"], w["g2"], tri, w["pq"], w["pk"], w["oq"], w["ok"]]
    return pl.pallas_call(
        kern,
        grid=(n // tm,),
        in_specs=[pl.BlockSpec((tm, d), lambda i: (i, 0))] + [_full(c.shape) for c in consts],
        out_specs=[
            pl.BlockSpec((n_heads, tm, HEAD_PAD), lambda i: (0, i, 0)),
            pl.BlockSpec((n_kv, tm, HEAD_PAD), lambda i: (0, i, 0)),
            pl.BlockSpec((n_kv, tm, HEAD_PAD), lambda i: (0, i, 0)),
            pl.BlockSpec((tm, n_kv * hd), lambda i: (i, 0)),
            pl.BlockSpec((tm, n_kv * hd), lambda i: (i, 0)),
            pl.BlockSpec((tm, n_gate), lambda i: (i, 0)),
            pl.BlockSpec((tm, LANES), lambda i: (i, 0)),
        ],
        out_shape=[
            jax.ShapeDtypeStruct((n_heads, n, HEAD_PAD), BF),
            jax.ShapeDtypeStruct((n_kv, n, HEAD_PAD), BF),
            jax.ShapeDtypeStruct((n_kv, n, HEAD_PAD), BF),
            jax.ShapeDtypeStruct((n, n_kv * hd), F32),
            jax.ShapeDtypeStruct((n, n_kv * hd), F32),
            jax.ShapeDtypeStruct((n, n_gate), F32),
            jax.ShapeDtypeStruct((n, LANES), F32),
        ],
        scratch_shapes=[pltpu.VMEM((1, LANES), F32)],
        compiler_params=_params(("arbitrary",)),
        name="fox_in",
    )(x, *consts)


def _fox_weights(w_in, b_f, g_q, g_k, g_attn, n_heads, n_kv, hd):
    d = w_in.shape[0]
    nq, nk = n_heads * hd, n_kv * hd
    n_gate = n_heads
    grp = n_heads // n_kv
    scale = hd ** -0.5
    pad = HEAD_PAD - hd

    def heads_padded(wm, nh):
        return jnp.pad(wm.reshape(d, nh, hd), ((0, 0), (0, 0), (0, pad))).reshape(d, nh * HEAD_PAD).astype(BF)

    ck0, cq0 = hd, hd + 3 * grp
    pq = np.zeros((LANES, n_heads * HEAD_PAD), np.float32)
    oq = np.zeros((1, n_heads * HEAD_PAD), np.float32)
    pk = np.zeros((LANES, n_kv * HEAD_PAD), np.float32)
    ok = np.zeros((1, n_kv * HEAD_PAD), np.float32)
    for h in range(n_heads):
        kv, g = divmod(h, grp)
        for p in range(3):
            pq[16 * p + h, HEAD_PAD * h + cq0 + p] = 1.0
            oq[0, HEAD_PAD * h + ck0 + 3 * g + p] = 1.0
            pk[16 * p + h, HEAD_PAD * kv + ck0 + 3 * g + p] = -1.0
    for kv in range(n_kv):
        for p in range(3):
            ok[0, HEAD_PAD * kv + cq0 + p] = 1.0
    g1 = np.zeros((HEAD_PAD, HEAD_PAD), np.float32)
    g1[:hd, :hd] = 1.0 / hd
    g2 = np.kron(np.eye(n_kv, dtype=np.float32), np.full((hd, hd), 1.0 / hd, np.float32))
    return dict(
        n_heads=n_heads, n_kv=n_kv, n_gate=n_gate, head_dim=hd,
        ga=g_attn.reshape(1, d),
        wq=heads_padded(w_in[:, :nq], n_heads),
        wk=heads_padded(w_in[:, nq:nq + nk], n_kv),
        wv=heads_padded(w_in[:, nq + nk:nq + 2 * nk], n_kv),
        wkv=w_in[:, nq:nq + 2 * nk].astype(BF),
        wf=jnp.pad(w_in[:, nq + 2 * nk:], ((0, 0), (0, LANES - n_gate))).astype(BF),
        bf=jnp.pad(b_f, (0, LANES - n_gate)).reshape(1, LANES),
        gq=jnp.pad(g_q * scale, (0, pad)).reshape(1, HEAD_PAD),
        gk=jnp.pad(g_k, (0, pad)).reshape(1, HEAD_PAD),
        gk2=jnp.tile(g_k, n_kv).reshape(1, nk),
        g1=jnp.asarray(g1, BF), g2=jnp.asarray(g2, BF),
        pq=jnp.asarray(pq, BF), pk=jnp.asarray(pk, BF), oq=jnp.asarray(oq), ok=jnp.asarray(ok),
    )


def _rope_lanes(x, cos, sin, lane, lo, half):
    nl = x.shape[-1]
    fwd = pltpu.roll(x, nl - half, 1)
    bwd = pltpu.roll(x, half, 1)
    first = (lane >= lo) & (lane < lo + half)
    second = (lane >= lo + half) & (lane < lo + 2 * half)
    rot = jnp.where(first, -fwd, jnp.where(second, bwd, 0.0))
    return jnp.where(first | second, x * cos + rot * sin, x)


def _mla_in_kernel(x_ref, cos_ref, sin_ref, ga_ref, win_ref, gcq_ref, gckv_ref, gkr_ref, wuq_ref, gq_ref,
                   wuk_ref, gk_ref, wuv_ref, g1_ref,
                   q_out, kk_out, vv_out, ckv_out, kpe_out,
                   *, n_heads, q_lora, kv_lora, nope, rope):
    tm = x_ref.shape[0]
    lane = lax.broadcasted_iota(jnp.int32, (tm, LANES), 1)
    cos, sin = cos_ref[...], sin_ref[...]
    xn = _rms_rows(x_ref[...], ga_ref[...]).astype(BF)
    z = _dot(xn, win_ref[...])
    cq = _rms_rows(z[:, :q_lora], gcq_ref[...]).astype(BF)
    ckv = _rms_rows(z[:, q_lora:q_lora + kv_lora], gckv_ref[...])
    ckv_out[...] = ckv
    ckv_b = ckv.astype(BF)
    kr = z[:, q_lora + kv_lora:]
    kr = kr * lax.rsqrt(jnp.sum(kr * kr, axis=-1, keepdims=True) * (1.0 / rope) + EPS) * gkr_ref[...]
    kpe = _rope_lanes(kr, cos, sin, lane, 0, rope // 2)
    kpe_out[...] = kpe[:, :rope]
    kpe_sh = pltpu.roll(kpe, nope, 1)

    g1 = g1_ref[...]
    for j in range(n_heads // 2):
        sl = slice(2 * HEAD_PAD * j, 2 * HEAD_PAD * (j + 1))
        zq = _dot(cq, wuq_ref[:, sl])
        zk = _dot(ckv_b, wuk_ref[:, sl])
        zv = _dot(ckv_b, wuv_ref[:, sl])
        for u in range(2):
            bl = slice(HEAD_PAD * u, HEAD_PAD * (u + 1))
            q = zq[:, bl]
            q = q * lax.rsqrt(_dot((q * q).astype(BF), g1) + EPS) * gq_ref[...]
            q_out[2 * j + u] = _rope_lanes(q, cos, sin, lane, nope, rope // 2).astype(BF)
            k = zk[:, bl]
            k = k * lax.rsqrt(_dot((k * k).astype(BF), g1) + EPS) * gk_ref[...]
            kk_out[2 * j + u] = (k + kpe_sh).astype(BF)
            vv_out[2 * j + u] = zv[:, bl].astype(BF)


def _mla_in(x, cos, sin, w, *, tm):
    n, d = x.shape
    n_heads, kv_lora, rope = w["n_heads"], w["kv_lora"], w["rope"]
    tbl_tiles = cos.shape[0] // tm
    kern = functools.partial(_mla_in_kernel, n_heads=n_heads, q_lora=w["q_lora"], kv_lora=kv_lora,
                             nope=w["nope"], rope=rope)
    consts = [w["ga"], w["win"], w["gcq"], w["gckv"], w["gkr"], w["wuq"], w["gq"], w["wuk"], w["gk"], w["wuv"], w["g1"]]
    head_spec = pl.BlockSpec((n_heads, tm, HEAD_PAD), lambda i: (0, i, 0))
    head_shape = jax.ShapeDtypeStruct((n_heads, n, HEAD_PAD), BF)
    return pl.pallas_call(
        kern,
        grid=(n // tm,),
        in_specs=[pl.BlockSpec((tm, d), lambda i: (i, 0)),
                  pl.BlockSpec((tm, LANES), lambda i: (i % tbl_tiles, 0)),
                  pl.BlockSpec((tm, LANES), lambda i: (i % tbl_tiles, 0))] + [_full(c.shape) for c in consts],
        out_specs=[head_spec, head_spec, head_spec,
                   pl.BlockSpec((tm, kv_lora), lambda i: (i, 0)),
                   pl.BlockSpec((tm, rope), lambda i: (i, 0))],
        out_shape=[head_shape, head_shape, head_shape,
                   jax.ShapeDtypeStruct((n, kv_lora), F32),
                   jax.ShapeDtypeStruct((n, rope), F32)],
        compiler_params=_params(("parallel",)),
        name="mla_in",
    )(x, cos, sin, *consts)


def _mla_weights(w_in, g_cq, w_uq, g_qn, g_qr, g_ckv, g_kr, w_ukv, g_kn, g_attn):
    d = w_in.shape[0]
    q_lora, kv_lora, rope, nope = g_cq.shape[0], g_ckv.shape[0], g_kr.shape[0], g_qn.shape[0]
    n_heads = w_uq.shape[1]
    vdim = w_ukv.shape[2] - nope
    scale = (nope + rope) ** -0.5
    in_pad = -w_in.shape[1] % LANES
    w_uk, w_uv = w_ukv[..., :nope], w_ukv[..., nope:]

    def heads_padded(wm):
        r, nh, hd = wm.shape
        return jnp.pad(wm, ((0, 0), (0, 0), (0, HEAD_PAD - hd))).reshape(r, nh * HEAD_PAD).astype(BF)

    g1 = np.zeros((HEAD_PAD, HEAD_PAD), np.float32)
    g1[:nope, :nope] = 1.0 / nope
    g1[nope:nope + rope, nope:nope + rope] = 1.0 / rope
    wuk_dh = jnp.transpose(w_uk, (0, 2, 1)).reshape(kv_lora, nope * n_heads).astype(BF)
    wabs = jnp.transpose(w_uk * g_kn[None, None, :], (1, 2, 0))
    wabs = jnp.pad(wabs, ((0, 0), (0, HEAD_PAD - nope), (0, 0))).reshape(n_heads * HEAD_PAD, kv_lora).astype(BF)
    return dict(
        n_heads=n_heads, q_lora=q_lora, kv_lora=kv_lora, rope=rope, nope=nope, vdim=vdim,
        ga=g_attn.reshape(1, d),
        win=jnp.pad(w_in, ((0, 0), (0, in_pad))).astype(BF),
        gcq=g_cq.reshape(1, -1), gckv=g_ckv.reshape(1, -1),
        gkr=jnp.pad(g_kr, (0, LANES - rope)).reshape(1, LANES),
        wuq=heads_padded(w_uq),
        gq=jnp.pad(jnp.concatenate([g_qn, g_qr]) * scale, (0, HEAD_PAD - nope - rope)).reshape(1, HEAD_PAD),
        wuk=heads_padded(w_uk), gk=jnp.pad(g_kn, (0, HEAD_PAD - nope)).reshape(1, HEAD_PAD),
        wuv=heads_padded(w_uv), g1=jnp.asarray(g1, BF),
        wuk_dh=wuk_dh, wabs=wabs, wuv_flat=w_uv.reshape(kv_lora, n_heads * vdim).astype(BF),
    )


def _rope_tables(pos, half):
    inv = ROPE_THETA ** (-jnp.arange(half, dtype=F32) / half)
    ang = pos.astype(F32)[:, None] * inv[None, :]
    reps = LANES // half
    return jnp.tile(jnp.cos(ang), (1, reps)), jnp.tile(jnp.sin(ang), (1, reps))


def _flash_kernel(qi_ref, ki_ref, q_ref, k_ref, v_ref, o_ref, m_ref, l_ref, acc_ref, *, n_q, n_k, vdim):
    step = pl.program_id(2)
    qi, ki = qi_ref[step], ki_ref[step]
    tq, tk = q_ref.shape[1], k_ref.shape[1]

    @pl.when(ki == 0)
    def _():
        m_ref[...] = jnp.full_like(m_ref, NEG)
        l_ref[...] = jnp.zeros_like(l_ref)
        acc_ref[...] = jnp.zeros_like(acc_ref)

    def block(masked):
        if masked:
            row = lax.broadcasted_iota(jnp.int32, (tq, tk), 0)
            col = lax.broadcasted_iota(jnp.int32, (tq, tk), 1)
            keep = col <= row
        for r in range(n_q):
            kv = r * n_k // n_q
            s = _dot_nt(q_ref[r], k_ref[kv])
            if masked:
                s = jnp.where(keep, s, NEG)
            m_prev = m_ref[r]
            m_new = jnp.maximum(m_prev, jnp.max(s, axis=-1, keepdims=True))
            alpha = jnp.exp(m_prev - m_new)
            p = jnp.exp(s - m_new)
            l_ref[r] = alpha * l_ref[r] + jnp.sum(p, axis=-1, keepdims=True)
            acc_ref[r] = alpha * acc_ref[r] + _dot(p.astype(BF), v_ref[kv])
            m_ref[r] = m_new

    pl.when(ki < qi)(lambda: block(False))

    @pl.when(ki == qi)
    def _():
        block(True)
        for j in range(n_q // 2):
            a = acc_ref[2 * j] / l_ref[2 * j]
            b = acc_ref[2 * j + 1] / l_ref[2 * j + 1]
            o_ref[:, LANES * j:LANES * (j + 1)] = (a + pltpu.roll(b, vdim, 1)).astype(o_ref.dtype)


def _flash_prompt(q, k, v, *, batch, n_q, n_k, tq, vdim):
    hq, n, _ = q.shape
    seq = n // batch
    nb = seq // tq
    qi = np.concatenate([np.full(i + 1, i) for i in range(nb)]).astype(np.int32)
    ki = np.concatenate([np.arange(i + 1) for i in range(nb)]).astype(np.int32)
    kern = functools.partial(_flash_kernel, n_q=n_q, n_k=n_k, vdim=vdim)
    grid_spec = pltpu.PrefetchScalarGridSpec(
        num_scalar_prefetch=2,
        grid=(batch, hq // n_q, len(qi)),
        in_specs=[
            pl.BlockSpec((n_q, tq, HEAD_PAD), lambda b, j, s, qt, kt: (j, b * nb + qt[s], 0)),
            pl.BlockSpec((n_k, tq, HEAD_PAD), lambda b, j, s, qt, kt: (j, b * nb + kt[s], 0)),
            pl.BlockSpec((n_k, tq, HEAD_PAD), lambda b, j, s, qt, kt: (j, b * nb + kt[s], 0)),
        ],
        out_specs=pl.BlockSpec((tq, n_q * vdim), lambda b, j, s, qt, kt: (b * nb + qt[s], j)),
        scratch_shapes=[pltpu.VMEM((n_q, tq, 1), F32), pltpu.VMEM((n_q, tq, 1), F32),
                        pltpu.VMEM((n_q, tq, HEAD_PAD), F32)],
    )
    return pl.pallas_call(
        kern, grid_spec=grid_spec,
        out_shape=jax.ShapeDtypeStruct((n, hq * vdim), BF),
        compiler_params=_params(("parallel", "parallel", "arbitrary")),
        name="flash_prompt",
    )(jnp.asarray(qi), jnp.asarray(ki), q, k, v)


def _out_ffn_kernel(h_ref, o_ref, wo_ref, gf_ref, wg_ref, wu_ref, wd_ref, out_ref, h1_ref, xn_ref, acc_ref):
    f = pl.program_id(1)

    @pl.when(f == 0)
    def _():
        h1 = h_ref[...] + _dot(o_ref[...].astype(BF), wo_ref[...])
        h1_ref[...] = h1
        xn_ref[...] = _rms_rows(h1, gf_ref[...]).astype(BF)
        acc_ref[...] = jnp.zeros_like(acc_ref)

    xn = xn_ref[...]
    g = _dot(xn, wg_ref[...])
    u = _dot(xn, wu_ref[...])
    acc_ref[...] += _dot((g * jax.nn.sigmoid(g) * u).astype(BF), wd_ref[...])

    @pl.when(f == pl.num_programs(1) - 1)
    def _():
        out_ref[...] = h1_ref[...] + acc_ref[...]


def _out_ffn(h, o, wo, gf, wg, wu, wd, *, tm, tf):
    n, d = h.shape
    do = o.shape[1]
    dff = wg.shape[1]
    return pl.pallas_call(
        _out_ffn_kernel,
        grid=(n // tm, dff // tf),
        in_specs=[
            pl.BlockSpec((tm, d), lambda i, f: (i, 0)),
            pl.BlockSpec((tm, do), lambda i, f: (i, 0)),
            pl.BlockSpec((do, d), lambda i, f: (0, 0)),
            pl.BlockSpec((1, d), lambda i, f: (0, 0)),
            pl.BlockSpec((d, tf), lambda i, f: (0, f)),
            pl.BlockSpec((d, tf), lambda i, f: (0, f)),
            pl.BlockSpec((tf, d), lambda i, f: (f, 0)),
        ],
        out_specs=pl.BlockSpec((tm, d), lambda i, f: (i, 0)),
        out_shape=jax.ShapeDtypeStruct((n, d), F32),
        scratch_shapes=[pltpu.VMEM((tm, d), F32), pltpu.VMEM((tm, d), BF), pltpu.VMEM((tm, d), F32)],
        compiler_params=_params(("parallel", "arbitrary")),
        name="out_ffn",
    )(h, o, wo, gf, wg, wu, wd)


def _softmax_step(s, v, m_ref, l_ref, acc_ref, v_is_transposed=False):
    m_prev = m_ref[...]
    m_new = jnp.maximum(m_prev, jnp.max(s, axis=-1, keepdims=True))
    alpha = jnp.exp(m_prev - m_new)
    p = jnp.exp(s - m_new)
    l_ref[...] = alpha * l_ref[...] + jnp.sum(p, axis=-1, keepdims=True)
    p = p.astype(BF)
    acc_ref[...] = alpha * acc_ref[...] + (_dot_nt(p, v) if v_is_transposed else _dot(p, v))
    m_ref[...] = m_new


def _rows_from_heads(x, t_new):
    return jnp.concatenate([jnp.broadcast_to(x[h:h + 1], (t_new, x.shape[1])) for h in range(x.shape[0])], axis=0)


def _fox_decode_kernel(pt_ref, q_ref, c_ref, k2_ref, v2_ref, u_ref, pg_ref, sg_ref, *rest,
                       pps, t_new, n_heads, n_kv, hd):
    kp, vp, lp = rest[:pps], rest[pps:2 * pps], rest[2 * pps:3 * pps]
    o_ref = rest[3 * pps]
    qbd_ref, cn_ref, m_ref, l_ref, acc_ref, carry_ref, kc_ref, vc_ref = rest[3 * pps + 1:]
    c = pl.program_id(1)
    rows = n_heads * t_new
    page = kp[0].shape[1]
    kvw = n_kv * hd
    row = lax.broadcasted_iota(jnp.int32, (rows, LANES), 0)
    lane = lax.broadcasted_iota(jnp.int32, (rows, LANES), 1)

    @pl.when(c == 0)
    def _():
        a = q_ref[...].astype(F32).reshape(rows, HEAD_PAD)
        a = jnp.where(lane < hd, a, 0.0)
        kvh = row // (rows // n_kv)
        for blk in range(kvw // LANES):
            parts = 0.0
            for u in range(LANES // hd):
                parts = parts + jnp.where(kvh == blk * (LANES // hd) + u, pltpu.roll(a, hd * u, 1) if u else a, 0.0)
            qbd_ref[:, LANES * blk:LANES * (blk + 1)] = parts.astype(BF)
        cn = c_ref[...]
        cn_rows = jnp.concatenate([cn] * n_heads, axis=0)
        cn_row = jnp.sum(jnp.where(lane == row // t_new, cn_rows, 0.0), axis=-1, keepdims=True)
        cn_ref[...] = cn_row
        m_ref[...] = jnp.full_like(m_ref, NEG)
        l_ref[...] = jnp.zeros_like(l_ref)
        acc_ref[...] = jnp.zeros_like(acc_ref)
        carry_ref[...] = jnp.zeros_like(carry_ref)
        zpad = jnp.zeros((page - t_new, kvw), F32)
        kn = jnp.concatenate([k2_ref[...], zpad], axis=0).astype(BF)
        vn = jnp.concatenate([v2_ref[...], zpad], axis=0).astype(BF)
        cn_pad = jnp.concatenate([cn, jnp.zeros((page - t_new, LANES), F32)], axis=0)
        cn_t = cn_pad.T[:n_heads]
        bias = cn_row - _rows_from_heads(cn_t, t_new)
        keep = (lane < t_new) & (lane <= row % t_new)
        s = jnp.where(keep, _dot_nt(qbd_ref[...], kn) + bias, NEG)
        _softmax_step(s, vn, m_ref, l_ref, acc_ref)

    carry = carry_ref[...]
    u_mat = u_ref[...]
    sfx = [None] * pps
    for p in reversed(range(pps)):
        lf_t = lp[p][...]
        sfx[p] = _dot_exact_lhs(lf_t, u_mat) + carry
        carry = carry + jnp.sum(lf_t, axis=-1, keepdims=True)
        kc_ref[:, page * p:page * (p + 1)] = kp[p][...].astype(BF)
        vc_ref[:, page * p:page * (p + 1)] = vp[p][...].astype(BF)
    carry_ref[...] = carry
    bias = _rows_from_heads(jnp.concatenate(sfx, axis=1), t_new) + cn_ref[...]
    s = _dot(qbd_ref[...], kc_ref[...]) + bias
    _softmax_step(s, vc_ref[...], m_ref, l_ref, acc_ref, v_is_transposed=True)

    @pl.when(c == pl.num_programs(1) - 1)
    def _():
        o = acc_ref[...] / l_ref[...]
        row2 = lax.broadcasted_iota(jnp.int32, (rows, kvw), 0)
        lane2 = lax.broadcasted_iota(jnp.int32, (rows, kvw), 1)
        om = jnp.where(lane2 // hd == row2 // (rows // n_kv), o, 0.0).astype(BF)
        out = jnp.zeros(o_ref.shape, F32)
        for g in range(n_heads // n_kv):
            out = out + _dot(sg_ref[g], _dot(om, pg_ref[g]).astype(BF))
        o_ref[...] = out


def _fox_decode(pt, q2, c2, k2, v2, cache_k, cache_v, cache_lf, layer, *, t_new, pps):
    n_heads, ns, _ = q2.shape
    b2 = ns // t_new
    _, n_pool, kvw, page = cache_k.shape
    n_gate = cache_lf.shape[2]
    n_pages = pt.shape[0] // b2
    nch = n_pages // pps
    hd = 64
    n_kv = kvw // hd
    grp = n_heads // n_kv
    rows = n_heads * t_new
    u_mat = jnp.asarray(np.tril(np.ones((page, page), np.float32), -1), BF)
    pg = np.zeros((grp, kvw, n_heads * hd), np.float32)
    sg = np.zeros((grp, t_new, rows), np.float32)
    for h in range(n_heads):
        kv, g = divmod(h, grp)
        for dd in range(hd):
            pg[g, kv * hd + dd, h * hd + dd] = 1.0
        for t in range(t_new):
            sg[g, t, h * t_new + t] = 1.0
    consts = [u_mat, jnp.asarray(pg, BF), jnp.asarray(sg, BF)]

    def page_map(p):
        return lambda b, c, ptr: (layer, ptr[b * n_pages + (nch - 1 - c) * pps + p], 0, 0)

    kern = functools.partial(_fox_decode_kernel, pps=pps, t_new=t_new, n_heads=n_heads, n_kv=n_kv, hd=hd)
    grid_spec = pltpu.PrefetchScalarGridSpec(
        num_scalar_prefetch=1,
        grid=(b2, nch),
        in_specs=[
            pl.BlockSpec((n_heads, t_new, HEAD_PAD), lambda b, c, ptr: (0, b, 0)),
            pl.BlockSpec((t_new, LANES), lambda b, c, ptr: (b, 0)),
            pl.BlockSpec((t_new, kvw), lambda b, c, ptr: (b, 0)),
            pl.BlockSpec((t_new, kvw), lambda b, c, ptr: (b, 0)),
        ] + [pl.BlockSpec(x.shape, lambda b, c, ptr, nd=x.ndim: (0,) * nd) for x in consts]
        + [pl.BlockSpec((None, None, kvw, page), page_map(p)) for p in range(pps)]
        + [pl.BlockSpec((None, None, kvw, page), page_map(p)) for p in range(pps)]
        + [pl.BlockSpec((None, None, n_gate, page), page_map(p)) for p in range(pps)],
        out_specs=pl.BlockSpec((t_new, n_heads * hd), lambda b, c, ptr: (b, 0)),
        scratch_shapes=[
            pltpu.VMEM((rows, kvw), BF), pltpu.VMEM((rows, 1), F32),
            pltpu.VMEM((rows, 1), F32), pltpu.VMEM((rows, 1), F32), pltpu.VMEM((rows, kvw), F32),
            pltpu.VMEM((n_gate, 1), F32),
            pltpu.VMEM((kvw, pps * page), BF), pltpu.VMEM((kvw, pps * page), BF),
        ],
    )
    return pl.pallas_call(
        kern, grid_spec=grid_spec,
        out_shape=jax.ShapeDtypeStruct((ns, n_heads * hd), F32),
        compiler_params=_params(("parallel", "arbitrary")),
        name="fox_decode",
    )(pt, q2, c2, k2, v2, *consts, *([cache_k] * pps), *([cache_v] * pps), *([cache_lf] * pps))


def _mla_decode_kernel(pt_ref, q_ref, c2_ref, r2_ref, wdh_ref, rq_ref, wabs_ref, wuv_ref, sel_ref, *rest,
                       pps, t_new, n_heads, nope, rope, vdim, kblk):
    cp, rp = rest[:pps], rest[pps:2 * pps]
    o_ref = rest[2 * pps]
    qa_ref, qr_ref, m_ref, l_ref, acc_ref, cc_ref, rc_ref, rn_ref = rest[2 * pps + 1:]
    c = pl.program_id(1)
    rows = n_heads * t_new
    page = cp[0].shape[0]
    row = lax.broadcasted_iota(jnp.int32, (rows, LANES), 0)
    lane = lax.broadcasted_iota(jnp.int32, (rows, LANES), 1)

    def nope_scores(n_keys):
        blk = min(kblk, n_keys)
        inv = []
        for kb in range(n_keys // blk):
            kf = _dot(cc_ref[kb * blk:(kb + 1) * blk, :], wdh_ref[...])
            sq = kf * kf
            part = sq[:, :LANES]
            for t in range(1, sq.shape[1] // LANES):
                part = part + sq[:, LANES * t:LANES * (t + 1)]
            hi = part.astype(BF)
            mid = (part - hi.astype(F32)).astype(BF)
            ssq = _dot_nt(rq_ref[...], hi) + _dot_nt(rq_ref[...], mid)
            inv.append(lax.rsqrt(ssq * (1.0 / nope) + EPS))
        inv = inv[0] if len(inv) == 1 else jnp.concatenate(inv, axis=1)
        return _dot_nt(qa_ref[...], cc_ref[:n_keys, :]) * inv

    @pl.when(c == 0)
    def _():
        a = q_ref[...].astype(F32).reshape(rows, HEAD_PAD)
        an = jnp.where(lane < nope, a, 0.0)
        head = row // t_new
        qbd = jnp.concatenate([jnp.where(head == h, an, 0.0) for h in range(n_heads)], axis=1).astype(BF)
        qa_ref[...] = _dot(qbd, wabs_ref[...]).astype(BF)
        qr_ref[...] = jnp.where(lane < rope, pltpu.roll(a, HEAD_PAD - nope, 1), 0.0).astype(BF)
        m_ref[...] = jnp.full_like(m_ref, NEG)
        l_ref[...] = jnp.zeros_like(l_ref)
        acc_ref[...] = jnp.zeros_like(acc_ref)
        cc_ref[:page, :] = jnp.concatenate(
            [c2_ref[...], jnp.zeros((page - t_new, c2_ref.shape[1]), F32)], axis=0).astype(BF)
        rn_ref[...] = jnp.zeros_like(rn_ref)
        rn_ref[:t_new, :rope] = r2_ref[...].astype(BF)
        keep = (lane < t_new) & (lane <= row % t_new)
        s = jnp.where(keep, nope_scores(page) + _dot_nt(qr_ref[...], rn_ref[...]), NEG)
        _softmax_step(s, cc_ref[:page, :], m_ref, l_ref, acc_ref)

    for p in range(pps):
        cc_ref[page * p:page * (p + 1), :] = cp[p][...].astype(BF)
        rc_ref[:, page * p:page * (p + 1)] = rp[p][...].astype(BF)
    s = nope_scores(pps * page) + _dot(qr_ref[:, :rope], rc_ref[...])
    _softmax_step(s, cc_ref[...], m_ref, l_ref, acc_ref)

    @pl.when(c == pl.num_programs(1) - 1)
    def _():
        olat = (acc_ref[...] / l_ref[...]).astype(BF)
        of = _dot(olat, wuv_ref[...])
        row2 = lax.broadcasted_iota(jnp.int32, of.shape, 0)
        lane2 = lax.broadcasted_iota(jnp.int32, of.shape, 1)
        om = jnp.where(lane2 // vdim == row2 // t_new, of, 0.0).astype(BF)
        o_ref[...] = _dot(sel_ref[...], om)


def _mla_decode(pt, q2, ckv2, kpe2, cache_c, cache_r, layer, w, *, t_new, pps):
    n_heads, ns, _ = q2.shape
    b2 = ns // t_new
    _, n_pool, page, kv_lora = cache_c.shape
    rope, nope, vdim = w["rope"], w["nope"], w["vdim"]
    n_pages = pt.shape[0] // b2
    nch = n_pages // pps
    rows = n_heads * t_new
    rq = np.zeros((rows, LANES), np.float32)
    sel = np.zeros((t_new, rows), np.float32)
    for r in range(rows):
        rq[r, np.arange(LANES) % n_heads == r // t_new] = 1.0
        sel[r % t_new, r] = 1.0
    consts = [w["wuk_dh"], jnp.asarray(rq, BF), w["wabs"], w["wuv_flat"], jnp.asarray(sel, BF)]

    def page_map(p):
        return lambda b, c, ptr: (layer, ptr[b * n_pages + c * pps + p], 0, 0)

    kern = functools.partial(_mla_decode_kernel, pps=pps, t_new=t_new, n_heads=n_heads, nope=nope, rope=rope,
                             vdim=vdim, kblk=4 * page)
    grid_spec = pltpu.PrefetchScalarGridSpec(
        num_scalar_prefetch=1,
        grid=(b2, nch),
        in_specs=[
            pl.BlockSpec((n_heads, t_new, HEAD_PAD), lambda b, c, ptr: (0, b, 0)),
            pl.BlockSpec((t_new, kv_lora), lambda b, c, ptr: (b, 0)),
            pl.BlockSpec((t_new, rope), lambda b, c, ptr: (b, 0)),
        ] + [pl.BlockSpec(x.shape, lambda b, c, ptr, nd=x.ndim: (0,) * nd) for x in consts]
        + [pl.BlockSpec((None, None, page, kv_lora), page_map(p)) for p in range(pps)]
        + [pl.BlockSpec((None, None, rope, page), page_map(p)) for p in range(pps)],
        out_specs=pl.BlockSpec((t_new, n_heads * vdim), lambda b, c, ptr: (b, 0)),
        scratch_shapes=[
            pltpu.VMEM((rows, kv_lora), BF), pltpu.VMEM((rows, LANES), BF),
            pltpu.VMEM((rows, 1), F32), pltpu.VMEM((rows, 1), F32), pltpu.VMEM((rows, kv_lora), F32),
            pltpu.VMEM((pps * page, kv_lora), BF), pltpu.VMEM((rope, pps * page), BF),
            pltpu.VMEM((page, LANES), BF),
        ],
    )
    return pl.pallas_call(
        kern, grid_spec=grid_spec,
        out_shape=jax.ShapeDtypeStruct((ns, n_heads * vdim), F32),
        compiler_params=_params(("parallel", "arbitrary")),
        name="mla_decode",
    )(pt, q2, ckv2, kpe2, *consts, *([cache_c] * pps), *([cache_r] * pps))


TM = 512
TQ = 512
PPS = 16


def _ffn_chunk(dff):
    for parts in (2, 1, 4, 11, 22):
        if dff % parts == 0 and (dff // parts) % LANES == 0:
            return dff // parts
    return dff


def kernel(x_prompt, x_sample, cache_fox_k, cache_fox_v, cache_fox_logf, cache_mla_ckv, cache_mla_kpe, page_table, attn_norm, ffn_norm, fox_w_in, fox_b_f, fox_g_q, fox_g_k, fox_w_o, mla_w_in, mla_g_cq, mla_w_uq, mla_g_qn, mla_g_qr, mla_g_ckv, mla_g_kr, mla_w_ukv, mla_g_kn, mla_w_o, ffn_w_gu, ffn_w_down):
    b1, seq, d = x_prompt.shape
    b2, t_new, _ = x_sample.shape
    depth = attn_norm.shape[0]
    n_fox, n_pool, page, fox_kv, fox_hd = cache_fox_k.shape
    fox_heads = fox_b_f.shape[1]
    past = page_table.shape[1] * page
    dff = ffn_w_down.shape[1]
    tf = _ffn_chunk(dff)
    n_p, n_s = b1 * seq, b2 * t_new
    tm_s = min(TM, n_s)
    assert fox_hd == 64 and seq % TQ == 0 and seq % TM == 0 and n_s % tm_s == 0 and tm_s % t_new == 0
    assert page_table.shape[1] % PPS == 0 and page == LANES and t_new == SUBLANES

    hp = x_prompt.reshape(n_p, d)
    hs = x_sample.reshape(n_s, d)
    pt = page_table.reshape(-1).astype(jnp.int32)
    ck4 = jnp.transpose(cache_fox_k, (0, 1, 3, 4, 2)).reshape(n_fox, n_pool, fox_kv * fox_hd, page)
    cv4 = jnp.transpose(cache_fox_v, (0, 1, 3, 4, 2)).reshape(n_fox, n_pool, fox_kv * fox_hd, page)
    clf = jnp.transpose(cache_fox_logf, (0, 1, 3, 2))
    ckr = jnp.transpose(cache_mla_kpe, (0, 1, 3, 2))

    tri_p = jnp.asarray(np.tril(np.ones((TM, TM), np.float32)), BF)
    seq_id = np.arange(tm_s) // t_new
    tri_s = jnp.asarray(np.tril(np.ones((tm_s, tm_s), np.float32)) * (seq_id[:, None] == seq_id[None, :]), BF)
    rope_half = mla_g_kr.shape[1] // 2
    cos_p, sin_p = _rope_tables(jnp.arange(seq), rope_half)
    cos_s, sin_s = _rope_tables(past + jnp.arange(tm_s) % t_new, rope_half)

    fk_p, fv_p, fl_p, fk_s, fv_s, fl_s = [], [], [], [], [], []
    mc_p, mr_p, mc_s, mr_s = [], [], [], []
    for i in range(depth):
        j = i // 2
        if i % 2 == 0:
            w = _fox_weights(fox_w_in[j], fox_b_f[j], fox_g_q[j], fox_g_k[j], attn_norm[i], fox_heads, fox_kv, fox_hd)
            q, ka, vp, k, v, lf, _ = _fox_in(hp, w, tm=TM, tiles_per_seq=seq // TM, tri=tri_p)
            o_p = _flash_prompt(q, ka, vp, batch=b1, n_q=fox_heads // fox_kv, n_k=1, tq=TQ, vdim=fox_hd)
            q2, _, _, k2, v2, lf2, c2 = _fox_in(hs, w, tm=tm_s, tiles_per_seq=1, tri=tri_s)
            o_s = _fox_decode(pt, q2, c2, k2, v2, ck4, cv4, clf, j, t_new=t_new, pps=PPS)
            fk_p.append(k); fv_p.append(v); fl_p.append(lf)
            fk_s.append(k2); fv_s.append(v2); fl_s.append(lf2)
            wo = fox_w_o[j].astype(BF)
        else:
            w = _mla_weights(mla_w_in[j], mla_g_cq[j], mla_w_uq[j], mla_g_qn[j], mla_g_qr[j], mla_g_ckv[j],
                             mla_g_kr[j], mla_w_ukv[j], mla_g_kn[j], attn_norm[i])
            q, kk, vv, ckv, kpe = _mla_in(hp, cos_p, sin_p, w, tm=TM)
            o_p = _flash_prompt(q, kk, vv, batch=b1, n_q=2, n_k=2, tq=TQ, vdim=w["vdim"])
            q2, _, _, ckv2, kpe2 = _mla_in(hs, cos_s, sin_s, w, tm=tm_s)
            o_s = _mla_decode(pt, q2, ckv2, kpe2, cache_mla_ckv, ckr, j, w, t_new=t_new, pps=PPS)
            mc_p.append(ckv); mr_p.append(kpe)
            mc_s.append(ckv2); mr_s.append(kpe2)
            wo = mla_w_o[j].astype(BF)
        gf = ffn_norm[i].reshape(1, d)
        wg = ffn_w_gu[i, :, :dff].astype(BF)
        wu = ffn_w_gu[i, :, dff:].astype(BF)
        wd = ffn_w_down[i].astype(BF)
        hp = _out_ffn(hp, o_p, wo, gf, wg, wu, wd, tm=TM, tf=tf)
        hs = _out_ffn(hs, o_s, wo, gf, wg, wu, wd, tm=tm_s, tf=tf)

    def stk(xs, lead, tail):
        return jnp.stack(xs).reshape((len(xs),) + lead + tail)

    return (hp.reshape(b1, seq, d), hs.reshape(b2, t_new, d),
            stk(fk_p, (b1, seq), (fox_kv, fox_hd)), stk(fv_p, (b1, seq), (fox_kv, fox_hd)),
            stk(fl_p, (b1, seq), (fox_heads,)),
            stk(mc_p, (b1, seq), (mla_g_ckv.shape[1],)), stk(mr_p, (b1, seq), (mla_g_kr.shape[1],)),
            stk(fk_s, (b2, t_new), (fox_kv, fox_hd)), stk(fv_s, (b2, t_new), (fox_kv, fox_hd)),
            stk(fl_s, (b2, t_new), (fox_heads,)),
            stk(mc_s, (b2, t_new), (mla_g_ckv.shape[1],)), stk(mr_s, (b2, t_new), (mla_g_kr.shape[1],)))
```

```python
import functools

import numpy as np
import jax
import jax.numpy as jnp
from jax import lax
from jax.experimental import pallas as pl
from jax.experimental.pallas import tpu as pltpu

BF = jnp.bfloat16
F32 = jnp.float32
EPS = 1e-6
ROPE_THETA = 10000.0
LANES = 128
SUBLANES = 8
HEAD_PAD = LANES
NEG = -1e30
LOG2E = 1.4426950408889634
VMEM_LIMIT = 56 * 1024 * 1024
NT = (((1,), (1,)), ((), ()))


def _dot(a, b):
    return jnp.dot(a, b, preferred_element_type=F32)


def _dot_nt(a, b):
    return lax.dot_general(a, b, NT, preferred_element_type=F32)


def _split3(x):
    hi = x.astype(BF)
    r1 = x - hi.astype(F32)
    mid = r1.astype(BF)
    lo = (r1 - mid.astype(F32)).astype(BF)
    return hi, mid, lo


def _dot_exact_lhs(x, w):
    hi, mid, lo = _split3(x)
    return _dot(hi, w) + _dot(mid, w) + _dot(lo, w)


def _dot_exact_rhs(w, x):
    hi, mid, lo = _split3(x)
    return _dot(w, hi) + _dot(w, mid) + _dot(w, lo)


def _rms_rows(x, g):
    return x * lax.rsqrt(jnp.mean(x * x, axis=-1, keepdims=True) + EPS) * g


def _log_sigmoid(x):
    return jnp.minimum(x, 0.0) - jnp.log1p(jnp.exp(-jnp.abs(x)))


def _full(shape):
    n = len(shape)
    return pl.BlockSpec(shape, lambda *_: (0,) * n)


def _params(sem):
    return pltpu.CompilerParams(dimension_semantics=sem, vmem_limit_bytes=VMEM_LIMIT)


def _fox_in_kernel(x_ref, ga_ref, wq_ref, wk_ref, wv_ref, wkv_ref, wf_ref, bf_ref, gq_ref, gk_ref, gk2_ref,
                   g1_ref, g2_ref, tri_ref, pq_ref, pk_ref, oq_ref, ok_ref,
                   q_out, ka_out, vp_out, k_out, v_out, lf_out, c_out, carry_ref,
                   *, tiles_per_seq, n_heads, n_kv, n_gate, hd):
    i = pl.program_id(0)
    tm = x_ref.shape[0]
    xn = _rms_rows(x_ref[...], ga_ref[...]).astype(BF)
    lane = lax.broadcasted_iota(jnp.int32, (tm, LANES), 1)

    lf = jnp.where(lane < n_gate, _log_sigmoid(_dot(xn, wf_ref[...]) + bf_ref[...]), 0.0)
    lf_out[...] = lf[:, :n_gate]

    @pl.when(i % tiles_per_seq == 0)
    def _():
        carry_ref[...] = jnp.zeros_like(carry_ref)

    c = _dot_exact_rhs(tri_ref[...], lf) + carry_ref[...]
    carry_ref[...] = c[tm - 1:tm, :]
    c = c * LOG2E
    c_out[...] = c

    ch, cm, cl = _split3(c)
    cparts = (ch.astype(F32) + pltpu.roll(cm.astype(F32), 16, 1) + pltpu.roll(cl.astype(F32), 32, 1)).astype(BF)

    g1 = g1_ref[...]
    v_ones = jnp.where(lane >= hd, 1.0, 0.0)
    for j in range(n_heads // 2):
        sl = slice(2 * HEAD_PAD * j, 2 * HEAD_PAD * (j + 1))
        zq = _dot(xn, wq_ref[:, sl])
        aug = _dot(cparts, pq_ref[:, sl]) + oq_ref[:, sl]
        for u in range(2):
            z = zq[:, HEAD_PAD * u:HEAD_PAD * (u + 1)]
            ms = _dot((z * z).astype(BF), g1)
            qn = z * lax.rsqrt(ms + EPS) * gq_ref[...]
            q_out[2 * j + u] = (qn + aug[:, HEAD_PAD * u:HEAD_PAD * (u + 1)]).astype(BF)

    for j in range(n_kv // 2):
        sl = slice(2 * HEAD_PAD * j, 2 * HEAD_PAD * (j + 1))
        zk = _dot(xn, wk_ref[:, sl])
        zv = _dot(xn, wv_ref[:, sl])
        aug = _dot(cparts, pk_ref[:, sl]) + ok_ref[:, sl]
        for u in range(2):
            z = zk[:, HEAD_PAD * u:HEAD_PAD * (u + 1)]
            ms = _dot((z * z).astype(BF), g1)
            kn = z * lax.rsqrt(ms + EPS) * gk_ref[...]
            ka_out[2 * j + u] = (kn + aug[:, HEAD_PAD * u:HEAD_PAD * (u + 1)]).astype(BF)
            vp_out[2 * j + u] = (zv[:, HEAD_PAD * u:HEAD_PAD * (u + 1)] + v_ones).astype(BF)

    zkv = _dot(xn, wkv_ref[...])
    nk = wkv_ref.shape[1] // 2
    k = zkv[:, :nk]
    ms = _dot((k * k).astype(BF), g2_ref[...])
    k_out[...] = k * lax.rsqrt(ms + EPS) * gk2_ref[...]
    v_out[...] = zkv[:, nk:]


def _fox_in(x, w, *, tm, tiles_per_seq, tri):
    n, d = x.shape
    n_heads, n_kv, n_gate, hd = w["n_heads"], w["n_kv"], w["n_gate"], w["head_dim"]
    kern = functools.partial(_fox_in_kernel, tiles_per_seq=tiles_per_seq, n_heads=n_heads, n_kv=n_kv, n_gate=n_gate,
                             hd=hd)
    consts = [w["ga"], w["wq"], w["wk"], w["wv"], w["wkv"], w["wf"], w["bf"], w["gq"], w["gk"], w["gk2"],
              w["g1"], w["g2"], tri, w["pq"], w["pk"], w["oq"], w["ok"]]
    return pl.pallas_call(
        kern,
        grid=(n // tm,),
        in_specs=[pl.BlockSpec((tm, d), lambda i: (i, 0))] + [_full(c.shape) for c in consts],
        out_specs=[
            pl.BlockSpec((n_heads, tm, HEAD_PAD), lambda i: (0, i, 0)),
            pl.BlockSpec((n_kv, tm, HEAD_PAD), lambda i: (0, i, 0)),
            pl.BlockSpec((n_kv, tm, HEAD_PAD), lambda i: (0, i, 0)),
            pl.BlockSpec((tm, n_kv * hd), lambda i: (i, 0)),
            pl.BlockSpec((tm, n_kv * hd), lambda i: (i, 0)),
            pl.BlockSpec((tm, n_gate), lambda i: (i, 0)),
            pl.BlockSpec((tm, LANES), lambda i: (i, 0)),
        ],
        out_shape=[
            jax.ShapeDtypeStruct((n_heads, n, HEAD_PAD), BF),
            jax.ShapeDtypeStruct((n_kv, n, HEAD_PAD), BF),
            jax.ShapeDtypeStruct((n_kv, n, HEAD_PAD), BF),
            jax.ShapeDtypeStruct((n, n_kv * hd), F32),
            jax.ShapeDtypeStruct((n, n_kv * hd), F32),
            jax.ShapeDtypeStruct((n, n_gate), F32),
            jax.ShapeDtypeStruct((n, LANES), F32),
        ],
        scratch_shapes=[pltpu.VMEM((1, LANES), F32)],
        compiler_params=_params(("arbitrary",)),
        name="fox_in",
    )(x, *consts)


def _fox_weights(w_in, b_f, g_q, g_k, g_attn, n_heads, n_kv, hd):
    d = w_in.shape[0]
    nq, nk = n_heads * hd, n_kv * hd
    n_gate = n_heads
    grp = n_heads // n_kv
    scale = hd ** -0.5
    pad = HEAD_PAD - hd

    def heads_padded(wm, nh):
        return jnp.pad(wm.reshape(d, nh, hd), ((0, 0), (0, 0), (0, pad))).reshape(d, nh * HEAD_PAD).astype(BF)

    ck0, cq0 = hd, hd + 3 * grp
    pq = np.zeros((LANES, n_heads * HEAD_PAD), np.float32)
    oq = np.zeros((1, n_heads * HEAD_PAD), np.float32)
    pk = np.zeros((LANES, n_kv * HEAD_PAD), np.float32)
    ok = np.zeros((1, n_kv * HEAD_PAD), np.float32)
    for h in range(n_heads):
        kv, g = divmod(h, grp)
        for p in range(3):
            pq[16 * p + h, HEAD_PAD * h + cq0 + p] = 1.0
            oq[0, HEAD_PAD * h + ck0 + 3 * g + p] = 1.0
            pk[16 * p + h, HEAD_PAD * kv + ck0 + 3 * g + p] = -1.0
    for kv in range(n_kv):
        for p in range(3):
            ok[0, HEAD_PAD * kv + cq0 + p] = 1.0
    g1 = np.zeros((HEAD_PAD, HEAD_PAD), np.float32)
    g1[:hd, :hd] = 1.0 / hd
    g2 = np.kron(np.eye(n_kv, dtype=np.float32), np.full((hd, hd), 1.0 / hd, np.float32))
    return dict(
        n_heads=n_heads, n_kv=n_kv, n_gate=n_gate, head_dim=hd,
        ga=g_attn.reshape(1, d),
        wq=heads_padded(w_in[:, :nq], n_heads),
        wk=heads_padded(w_in[:, nq:nq + nk], n_kv),
        wv=heads_padded(w_in[:, nq + nk:nq + 2 * nk], n_kv),
        wkv=w_in[:, nq:nq + 2 * nk].astype(BF),
        wf=jnp.pad(w_in[:, nq + 2 * nk:], ((0, 0), (0, LANES - n_gate))).astype(BF),
        bf=jnp.pad(b_f, (0, LANES - n_gate)).reshape(1, LANES),
        gq=jnp.pad(g_q * (scale * LOG2E), (0, pad)).reshape(1, HEAD_PAD),
        gk=jnp.pad(g_k, (0, pad)).reshape(1, HEAD_PAD),
        gk2=jnp.tile(g_k, n_kv).reshape(1, nk),
        g1=jnp.asarray(g1, BF), g2=jnp.asarray(g2, BF),
        pq=jnp.asarray(pq, BF), pk=jnp.asarray(pk, BF), oq=jnp.asarray(oq), ok=jnp.asarray(ok),
    )


def _rope_lanes(x, cos, sin, lane, lo, half):
    nl = x.shape[-1]
    fwd = pltpu.roll(x, nl - half, 1)
    bwd = pltpu.roll(x, half, 1)
    first = (lane >= lo) & (lane < lo + half)
    second = (lane >= lo + half) & (lane < lo + 2 * half)
    rot = jnp.where(first, -fwd, jnp.where(second, bwd, 0.0))
    return jnp.where(first | second, x * cos + rot * sin, x)


def _mla_in_kernel(x_ref, cos_ref, sin_ref, ga_ref, win_ref, gcq_ref, gckv_ref, gkr_ref, wuq_ref, gq_ref,
                   wuk_ref, gk_ref, wuv_ref, g1_ref,
                   q_out, kk_out, vv_out, ckv_out, kpe_out,
                   *, n_heads, q_lora, kv_lora, nope, rope, vdim):
    tm = x_ref.shape[0]
    lane = lax.broadcasted_iota(jnp.int32, (tm, LANES), 1)
    cos, sin = cos_ref[...], sin_ref[...]
    xn = _rms_rows(x_ref[...], ga_ref[...]).astype(BF)
    z = _dot(xn, win_ref[...])
    cq = _rms_rows(z[:, :q_lora], gcq_ref[...]).astype(BF)
    ckv = _rms_rows(z[:, q_lora:q_lora + kv_lora], gckv_ref[...])
    ckv_out[...] = ckv
    ckv_b = ckv.astype(BF)
    kr = z[:, q_lora + kv_lora:]
    kr = kr * lax.rsqrt(jnp.sum(kr * kr, axis=-1, keepdims=True) * (1.0 / rope) + EPS) * gkr_ref[...]
    kpe = _rope_lanes(kr, cos, sin, lane, 0, rope // 2)
    kpe_out[...] = kpe[:, :rope]
    kpe_sh = pltpu.roll(kpe, nope, 1)

    g1 = g1_ref[...]
    v_ones = jnp.where(lane >= vdim, 1.0, 0.0)
    for j in range(n_heads // 2):
        sl = slice(2 * HEAD_PAD * j, 2 * HEAD_PAD * (j + 1))
        zq = _dot(cq, wuq_ref[:, sl])
        zk = _dot(ckv_b, wuk_ref[:, sl])
        zv = _dot(ckv_b, wuv_ref[:, sl])
        for u in range(2):
            bl = slice(HEAD_PAD * u, HEAD_PAD * (u + 1))
            q = zq[:, bl]
            q = q * lax.rsqrt(_dot((q * q).astype(BF), g1) + EPS) * gq_ref[...]
            q_out[2 * j + u] = _rope_lanes(q, cos, sin, lane, nope, rope // 2).astype(BF)
            k = zk[:, bl]
            k = k * lax.rsqrt(_dot((k * k).astype(BF), g1) + EPS) * gk_ref[...]
            kk_out[2 * j + u] = (k + kpe_sh).astype(BF)
            vv_out[2 * j + u] = (zv[:, bl] + v_ones).astype(BF)


def _mla_in(x, cos, sin, w, *, tm):
    n, d = x.shape
    n_heads, kv_lora, rope = w["n_heads"], w["kv_lora"], w["rope"]
    tbl_tiles = cos.shape[0] // tm
    kern = functools.partial(_mla_in_kernel, n_heads=n_heads, q_lora=w["q_lora"], kv_lora=kv_lora,
                             nope=w["nope"], rope=rope, vdim=w["vdim"])
    consts = [w["ga"], w["win"], w["gcq"], w["gckv"], w["gkr"], w["wuq"], w["gq"], w["wuk"], w["gk"], w["wuv"], w["g1"]]
    head_spec = pl.BlockSpec((n_heads, tm, HEAD_PAD), lambda i: (0, i, 0))
    head_shape = jax.ShapeDtypeStruct((n_heads, n, HEAD_PAD), BF)
    return pl.pallas_call(
        kern,
        grid=(n // tm,),
        in_specs=[pl.BlockSpec((tm, d), lambda i: (i, 0)),
                  pl.BlockSpec((tm, LANES), lambda i: (i % tbl_tiles, 0)),
                  pl.BlockSpec((tm, LANES), lambda i: (i % tbl_tiles, 0))] + [_full(c.shape) for c in consts],
        out_specs=[head_spec, head_spec, head_spec,
                   pl.BlockSpec((tm, kv_lora), lambda i: (i, 0)),
                   pl.BlockSpec((tm, rope), lambda i: (i, 0))],
        out_shape=[head_shape, head_shape, head_shape,
                   jax.ShapeDtypeStruct((n, kv_lora), F32),
                   jax.ShapeDtypeStruct((n, rope), F32)],
        compiler_params=_params(("parallel",)),
        name="mla_in",
    )(x, cos, sin, *consts)


def _mla_weights(w_in, g_cq, w_uq, g_qn, g_qr, g_ckv, g_kr, w_ukv, g_kn, g_attn):
    d = w_in.shape[0]
    q_lora, kv_lora, rope, nope = g_cq.shape[0], g_ckv.shape[0], g_kr.shape[0], g_qn.shape[0]
    n_heads = w_uq.shape[1]
    vdim = w_ukv.shape[2] - nope
    scale = (nope + rope) ** -0.5
    in_pad = -w_in.shape[1] % LANES
    w_uk, w_uv = w_ukv[..., :nope], w_ukv[..., nope:]

    def heads_padded(wm):
        r, nh, hd = wm.shape
        return jnp.pad(wm, ((0, 0), (0, 0), (0, HEAD_PAD - hd))).reshape(r, nh * HEAD_PAD).astype(BF)

    g1 = np.zeros((HEAD_PAD, HEAD_PAD), np.float32)
    g1[:nope, :nope] = 1.0 / nope
    g1[nope:nope + rope, nope:nope + rope] = 1.0 / rope
    wuk_dh = jnp.transpose(w_uk, (0, 2, 1)).reshape(kv_lora, nope * n_heads).astype(BF)
    wabs = jnp.transpose(w_uk * g_kn[None, None, :], (1, 2, 0))
    wabs = jnp.pad(wabs, ((0, 0), (0, HEAD_PAD - nope), (0, 0))).reshape(n_heads * HEAD_PAD, kv_lora).astype(BF)
    return dict(
        n_heads=n_heads, q_lora=q_lora, kv_lora=kv_lora, rope=rope, nope=nope, vdim=vdim,
        ga=g_attn.reshape(1, d),
        win=jnp.pad(w_in, ((0, 0), (0, in_pad))).astype(BF),
        gcq=g_cq.reshape(1, -1), gckv=g_ckv.reshape(1, -1),
        gkr=jnp.pad(g_kr, (0, LANES - rope)).reshape(1, LANES),
        wuq=heads_padded(w_uq),
        gq=jnp.pad(jnp.concatenate([g_qn, g_qr]) * (scale * LOG2E), (0, HEAD_PAD - nope - rope)).reshape(1, HEAD_PAD),
        wuk=heads_padded(w_uk), gk=jnp.pad(g_kn, (0, HEAD_PAD - nope)).reshape(1, HEAD_PAD),
        wuv=heads_padded(w_uv), g1=jnp.asarray(g1, BF),
        wuk_dh=wuk_dh, wabs=wabs, wuv_flat=w_uv.reshape(kv_lora, n_heads * vdim).astype(BF),
    )


def _rope_tables(pos, half):
    inv = ROPE_THETA ** (-jnp.arange(half, dtype=F32) / half)
    ang = pos.astype(F32)[:, None] * inv[None, :]
    reps = LANES // half
    return jnp.tile(jnp.cos(ang), (1, reps)), jnp.tile(jnp.sin(ang), (1, reps))


def _flash_kernel(q_ref, k_ref, v_ref, o_ref, s_ref, m_ref, acc_ref, *, n_q, n_k, vdim, tkc):
    qi = pl.program_id(2)
    tq = q_ref.shape[1]
    diag = (qi * tq) // tkc
    off = qi * tq - diag * tkc
    lane_tiles = tkc // LANES

    m_ref[...] = jnp.full_like(m_ref, NEG)
    acc_ref[...] = jnp.zeros_like(acc_ref)

    def lane_max(s):
        m = s[:, :LANES]
        for t in range(1, lane_tiles):
            m = jnp.maximum(m, s[:, LANES * t:LANES * (t + 1)])
        return m

    def score_chunk(kc, keep):
        k0 = pl.multiple_of(kc * tkc, tkc)
        for r in range(n_q):
            s = _dot_nt(q_ref[r], k_ref[r * n_k // n_q, pl.ds(k0, tkc), :])
            if keep is not None:
                s = jnp.where(keep, s, NEG)
            s_ref[r, kc] = s
            m_ref[r] = jnp.maximum(m_ref[r], lane_max(s))

    def pass_a(kc, carry):
        score_chunk(kc, None)
        return carry

    lax.fori_loop(0, diag, pass_a, 0)
    row = lax.broadcasted_iota(jnp.int32, (tq, tkc), 0)
    col = lax.broadcasted_iota(jnp.int32, (tq, tkc), 1)
    score_chunk(diag, col <= row + off)

    for r in range(n_q):
        m_ref[r] = jnp.broadcast_to(jnp.max(m_ref[r], axis=-1, keepdims=True), (tq, LANES))

    def pass_b(kc, carry):
        k0 = pl.multiple_of(kc * tkc, tkc)
        for r in range(n_q):
            m = m_ref[r]
            p = jnp.exp2(s_ref[r, kc] - jnp.concatenate([m] * lane_tiles, axis=1)).astype(BF)
            acc_ref[r] += _dot(p, v_ref[r * n_k // n_q, pl.ds(k0, tkc), :])
        return carry

    lax.fori_loop(0, diag + 1, pass_b, 0)

    lane = lax.broadcasted_iota(jnp.int32, (tq, LANES), 1)

    def normalised(r):
        a = acc_ref[r]
        return jnp.where(lane < vdim, a / jnp.where(lane < vdim, pltpu.roll(a, vdim, 1), 1.0), 0.0)

    for j in range(n_q // 2):
        o_ref[:, LANES * j:LANES * (j + 1)] = (
            normalised(2 * j) + pltpu.roll(normalised(2 * j + 1), vdim, 1)).astype(o_ref.dtype)


def _flash_prompt(q, k, v, *, batch, n_q, n_k, tq, tkc, vdim):
    hq, n, _ = q.shape
    seq = n // batch
    nb = seq // tq
    assert tkc % tq == 0 and seq % tkc == 0 and 2 * vdim == LANES
    kern = functools.partial(_flash_kernel, n_q=n_q, n_k=n_k, vdim=vdim, tkc=tkc)
    return pl.pallas_call(
        kern,
        grid=(batch, hq // n_q, nb),
        in_specs=[
            pl.BlockSpec((n_q, tq, HEAD_PAD), lambda b, j, i: (j, b * nb + i, 0)),
            pl.BlockSpec((n_k, seq, HEAD_PAD), lambda b, j, i: (j, b, 0)),
            pl.BlockSpec((n_k, seq, HEAD_PAD), lambda b, j, i: (j, b, 0)),
        ],
        out_specs=pl.BlockSpec((tq, n_q * vdim), lambda b, j, i: (b * nb + i, j)),
        out_shape=jax.ShapeDtypeStruct((n, hq * vdim), BF),
        scratch_shapes=[pltpu.VMEM((n_q, seq // tkc, tq, tkc), F32), pltpu.VMEM((n_q, tq, LANES), F32),
                        pltpu.VMEM((n_q, tq, HEAD_PAD), F32)],
        compiler_params=_params(("parallel", "parallel", "arbitrary")),
        name="flash_prompt",
    )(q, k, v)


def _out_ffn_kernel(h_ref, o_ref, wo_ref, gf_ref, wg_ref, wu_ref, wd_ref, out_ref, h1_ref, xn_ref, acc_ref):
    f = pl.program_id(1)

    @pl.when(f == 0)
    def _():
        h1 = h_ref[...] + _dot(o_ref[...].astype(BF), wo_ref[...])
        h1_ref[...] = h1
        xn_ref[...] = _rms_rows(h1, gf_ref[...]).astype(BF)
        acc_ref[...] = jnp.zeros_like(acc_ref)

    xn = xn_ref[...]
    g = _dot(xn, wg_ref[...])
    u = _dot(xn, wu_ref[...])
    acc_ref[...] += _dot((g * jax.nn.sigmoid(g) * u).astype(BF), wd_ref[...])

    @pl.when(f == pl.num_programs(1) - 1)
    def _():
        out_ref[...] = h1_ref[...] + acc_ref[...]


def _out_ffn(h, o, wo, gf, wg, wu, wd, *, tm, tf):
    n, d = h.shape
    do = o.shape[1]
    dff = wg.shape[1]
    return pl.pallas_call(
        _out_ffn_kernel,
        grid=(n // tm, dff // tf),
        in_specs=[
            pl.BlockSpec((tm, d), lambda i, f: (i, 0)),
            pl.BlockSpec((tm, do), lambda i, f: (i, 0)),
            pl.BlockSpec((do, d), lambda i, f: (0, 0)),
            pl.BlockSpec((1, d), lambda i, f: (0, 0)),
            pl.BlockSpec((d, tf), lambda i, f: (0, f)),
            pl.BlockSpec((d, tf), lambda i, f: (0, f)),
            pl.BlockSpec((tf, d), lambda i, f: (f, 0)),
        ],
        out_specs=pl.BlockSpec((tm, d), lambda i, f: (i, 0)),
        out_shape=jax.ShapeDtypeStruct((n, d), F32),
        scratch_shapes=[pltpu.VMEM((tm, d), F32), pltpu.VMEM((tm, d), BF), pltpu.VMEM((tm, d), F32)],
        compiler_params=_params(("parallel", "arbitrary")),
        name="out_ffn",
    )(h, o, wo, gf, wg, wu, wd)


def _softmax_step(s, v, m_ref, l_ref, acc_ref, v_is_transposed=False):
    m_prev = m_ref[...]
    m_new = jnp.maximum(m_prev, jnp.max(s, axis=-1, keepdims=True))
    alpha = jnp.exp2(m_prev - m_new)
    p = jnp.exp2(s - m_new)
    l_ref[...] = alpha * l_ref[...] + jnp.sum(p, axis=-1, keepdims=True)
    p = p.astype(BF)
    acc_ref[...] = alpha * acc_ref[...] + (_dot_nt(p, v) if v_is_transposed else _dot(p, v))
    m_ref[...] = m_new


def _rows_from_heads(x, t_new):
    return jnp.concatenate([jnp.broadcast_to(x[h:h + 1], (t_new, x.shape[1])) for h in range(x.shape[0])], axis=0)


def _chunk_pipeline(pt_ref, srcs, bufs, sems, *, layer, n_pages, pps, nch, reverse, compute):
    b = pl.program_id(0)
    n_seq = pl.num_programs(0)

    def chunk_copies(seq, c, slot):
        first = ((nch - 1 - c) if reverse else c) * pps
        out = []
        for p in range(pps):
            pid = pt_ref[seq * n_pages + first + p]
            for i, (src, buf) in enumerate(zip(srcs, bufs)):
                out.append(pltpu.make_async_copy(src.at[layer, pid], buf.at[slot, p], sems.at[i, slot]))
        return out

    @pl.when(b == 0)
    def _():
        for cp in chunk_copies(b, 0, 0):
            cp.start()

    for c in range(nch):
        slot = c % 2
        if c + 1 < nch:
            for cp in chunk_copies(b, c + 1, 1 - slot):
                cp.start()
        else:
            @pl.when(b + 1 < n_seq)
            def _():
                for cp in chunk_copies(b + 1, 0, 1 - slot):
                    cp.start()
        for cp in chunk_copies(b, c, slot):
            cp.wait()
        compute(c, slot)


def _fox_decode_kernel(pt_ref, q_ref, c_ref, k2_ref, v2_ref, u_ref, pg_ref, sg_ref, ck_hbm, cv_hbm, cl_hbm,
                       o_ref, kbuf, vbuf, lbuf, sems, qbd_ref, m_ref, l_ref, acc_ref, kc_ref, vc_ref,
                       *, layer, n_pages, pps, nch, t_new, n_heads, n_kv, hd):
    rows = n_heads * t_new
    page = kbuf.shape[3]
    kvw = n_kv * hd
    row = lax.broadcasted_iota(jnp.int32, (rows, LANES), 0)
    lane = lax.broadcasted_iota(jnp.int32, (rows, LANES), 1)

    a = q_ref[...].astype(F32).reshape(rows, HEAD_PAD)
    a = jnp.where(lane < hd, a, 0.0)
    kvh = row // (rows // n_kv)
    for blk in range(kvw // LANES):
        parts = 0.0
        for u in range(LANES // hd):
            parts = parts + jnp.where(kvh == blk * (LANES // hd) + u, pltpu.roll(a, hd * u, 1) if u else a, 0.0)
        qbd_ref[:, LANES * blk:LANES * (blk + 1)] = parts.astype(BF)
    cn = c_ref[...]
    cn_rows = jnp.concatenate([cn] * n_heads, axis=0)
    cn_row = jnp.sum(jnp.where(lane == row // t_new, cn_rows, 0.0), axis=-1, keepdims=True)
    m_ref[...] = jnp.full_like(m_ref, NEG)
    l_ref[...] = jnp.zeros_like(l_ref)
    acc_ref[...] = jnp.zeros_like(acc_ref)
    zpad = jnp.zeros((page - t_new, kvw), F32)
    kn = jnp.concatenate([k2_ref[...], zpad], axis=0).astype(BF)
    vn = jnp.concatenate([v2_ref[...], zpad], axis=0).astype(BF)
    cn_pad = jnp.concatenate([cn, jnp.zeros((page - t_new, LANES), F32)], axis=0)
    cn_t = cn_pad.T[:n_heads]
    bias = cn_row - _rows_from_heads(cn_t, t_new)
    keep = (lane < t_new) & (lane <= row % t_new)
    s = jnp.where(keep, _dot_nt(qbd_ref[...], kn) + bias, NEG)
    _softmax_step(s, vn, m_ref, l_ref, acc_ref)

    u_mat = u_ref[...]
    carry = [jnp.zeros((n_heads, 1), F32)]

    def compute(c, slot):
        sfx = [None] * pps
        for p in reversed(range(pps)):
            lf_t = lbuf[slot, p]
            sfx[p] = _dot_exact_lhs(lf_t, u_mat) + carry[0]
            carry[0] = carry[0] + jnp.sum(lf_t, axis=-1, keepdims=True)
            kc_ref[:, page * p:page * (p + 1)] = kbuf[slot, p].astype(BF)
            vc_ref[:, page * p:page * (p + 1)] = vbuf[slot, p].astype(BF)
        bias = _rows_from_heads(jnp.concatenate(sfx, axis=1) * LOG2E, t_new) + cn_row
        s = _dot(qbd_ref[...], kc_ref[...]) + bias
        _softmax_step(s, vc_ref[...], m_ref, l_ref, acc_ref, v_is_transposed=True)

    _chunk_pipeline(pt_ref, (ck_hbm, cv_hbm, cl_hbm), (kbuf, vbuf, lbuf), sems, layer=layer, n_pages=n_pages,
                    pps=pps, nch=nch, reverse=True, compute=compute)

    o = acc_ref[...] / l_ref[...]
    row2 = lax.broadcasted_iota(jnp.int32, (rows, kvw), 0)
    lane2 = lax.broadcasted_iota(jnp.int32, (rows, kvw), 1)
    om = jnp.where(lane2 // hd == row2 // (rows // n_kv), o, 0.0).astype(BF)
    out = jnp.zeros(o_ref.shape, F32)
    for g in range(n_heads // n_kv):
        out = out + _dot(sg_ref[g], _dot(om, pg_ref[g]).astype(BF))
    o_ref[...] = out


def _fox_decode(pt, q2, c2, k2, v2, cache_k, cache_v, cache_lf, layer, *, t_new, pps):
    n_heads, ns, _ = q2.shape
    b2 = ns // t_new
    _, n_pool, kvw, page = cache_k.shape
    n_gate = cache_lf.shape[2]
    n_pages = pt.shape[0] // b2
    nch = n_pages // pps
    assert nch % 2 == 0
    hd = 64
    n_kv = kvw // hd
    grp = n_heads // n_kv
    rows = n_heads * t_new
    u_mat = jnp.asarray(np.tril(np.ones((page, page), np.float32), -1), BF)
    pg = np.zeros((grp, kvw, n_heads * hd), np.float32)
    sg = np.zeros((grp, t_new, rows), np.float32)
    for h in range(n_heads):
        kv, g = divmod(h, grp)
        for dd in range(hd):
            pg[g, kv * hd + dd, h * hd + dd] = 1.0
        for t in range(t_new):
            sg[g, t, h * t_new + t] = 1.0
    consts = [u_mat, jnp.asarray(pg, BF), jnp.asarray(sg, BF)]
    kern = functools.partial(_fox_decode_kernel, layer=layer, n_pages=n_pages, pps=pps, nch=nch, t_new=t_new,
                             n_heads=n_heads, n_kv=n_kv, hd=hd)
    grid_spec = pltpu.PrefetchScalarGridSpec(
        num_scalar_prefetch=1,
        grid=(b2,),
        in_specs=[
            pl.BlockSpec((n_heads, t_new, HEAD_PAD), lambda b, ptr: (0, b, 0)),
            pl.BlockSpec((t_new, LANES), lambda b, ptr: (b, 0)),
            pl.BlockSpec((t_new, kvw), lambda b, ptr: (b, 0)),
            pl.BlockSpec((t_new, kvw), lambda b, ptr: (b, 0)),
        ] + [pl.BlockSpec(x.shape, lambda b, ptr, nd=x.ndim: (0,) * nd) for x in consts]
        + [pl.BlockSpec(memory_space=pl.ANY)] * 3,
        out_specs=pl.BlockSpec((t_new, n_heads * hd), lambda b, ptr: (b, 0)),
        scratch_shapes=[
            pltpu.VMEM((2, pps, kvw, page), F32), pltpu.VMEM((2, pps, kvw, page), F32),
            pltpu.VMEM((2, pps, n_gate, page), F32), pltpu.SemaphoreType.DMA((3, 2)),
            pltpu.VMEM((rows, kvw), BF),
            pltpu.VMEM((rows, 1), F32), pltpu.VMEM((rows, 1), F32), pltpu.VMEM((rows, kvw), F32),
            pltpu.VMEM((kvw, pps * page), BF), pltpu.VMEM((kvw, pps * page), BF),
        ],
    )
    return pl.pallas_call(
        kern, grid_spec=grid_spec,
        out_shape=jax.ShapeDtypeStruct((ns, n_heads * hd), F32),
        compiler_params=_params(("arbitrary",)),
        name="fox_decode",
    )(pt, q2, c2, k2, v2, *consts, cache_k, cache_v, cache_lf)


def _mla_decode_kernel(pt_ref, q_ref, c2_ref, r2_ref, wdh_ref, rq_ref, wabs_ref, wuv_ref, sel_ref, cc_hbm, cr_hbm,
                       o_ref, cbuf, rbuf, sems, qa_ref, qr_ref, m_ref, l_ref, acc_ref, cc_ref, rc_ref, rn_ref,
                       *, layer, n_pages, pps, nch, t_new, n_heads, nope, rope, vdim, kblk):
    rows = n_heads * t_new
    page = cbuf.shape[2]
    row = lax.broadcasted_iota(jnp.int32, (rows, LANES), 0)
    lane = lax.broadcasted_iota(jnp.int32, (rows, LANES), 1)

    def nope_scores(n_keys):
        blk = min(kblk, n_keys)
        inv = []
        for kb in range(n_keys // blk):
            kf = _dot(cc_ref[kb * blk:(kb + 1) * blk, :], wdh_ref[...])
            sq = kf * kf
            part = sq[:, :LANES]
            for t in range(1, sq.shape[1] // LANES):
                part = part + sq[:, LANES * t:LANES * (t + 1)]
            hi = part.astype(BF)
            mid = (part - hi.astype(F32)).astype(BF)
            ssq = _dot_nt(rq_ref[...], hi) + _dot_nt(rq_ref[...], mid)
            inv.append(lax.rsqrt(ssq * (1.0 / nope) + EPS))
        inv = inv[0] if len(inv) == 1 else jnp.concatenate(inv, axis=1)
        return _dot_nt(qa_ref[...], cc_ref[:n_keys, :]) * inv

    a = q_ref[...].astype(F32).reshape(rows, HEAD_PAD)
    an = jnp.where(lane < nope, a, 0.0)
    head = row // t_new
    qbd = jnp.concatenate([jnp.where(head == h, an, 0.0) for h in range(n_heads)], axis=1).astype(BF)
    qa_ref[...] = _dot(qbd, wabs_ref[...]).astype(BF)
    qr_ref[...] = jnp.where(lane < rope, pltpu.roll(a, HEAD_PAD - nope, 1), 0.0).astype(BF)
    m_ref[...] = jnp.full_like(m_ref, NEG)
    l_ref[...] = jnp.zeros_like(l_ref)
    acc_ref[...] = jnp.zeros_like(acc_ref)
    cc_ref[:page, :] = jnp.concatenate(
        [c2_ref[...], jnp.zeros((page - t_new, c2_ref.shape[1]), F32)], axis=0).astype(BF)
    rn_ref[...] = jnp.zeros_like(rn_ref)
    rn_ref[:t_new, :rope] = r2_ref[...].astype(BF)
    keep = (lane < t_new) & (lane <= row % t_new)
    s = jnp.where(keep, nope_scores(page) + _dot_nt(qr_ref[...], rn_ref[...]), NEG)
    _softmax_step(s, cc_ref[:page, :], m_ref, l_ref, acc_ref)

    def compute(c, slot):
        for p in range(pps):
            cc_ref[page * p:page * (p + 1), :] = cbuf[slot, p].astype(BF)
            rc_ref[:, page * p:page * (p + 1)] = rbuf[slot, p].astype(BF)
        s = nope_scores(pps * page) + _dot(qr_ref[:, :rope], rc_ref[...])
        _softmax_step(s, cc_ref[...], m_ref, l_ref, acc_ref)

    _chunk_pipeline(pt_ref, (cc_hbm, cr_hbm), (cbuf, rbuf), sems, layer=layer, n_pages=n_pages,
                    pps=pps, nch=nch, reverse=False, compute=compute)

    olat = (acc_ref[...] / l_ref[...]).astype(BF)
    of = _dot(olat, wuv_ref[...])
    row2 = lax.broadcasted_iota(jnp.int32, of.shape, 0)
    lane2 = lax.broadcasted_iota(jnp.int32, of.shape, 1)
    om = jnp.where(lane2 // vdim == row2 // t_new, of, 0.0).astype(BF)
    o_ref[...] = _dot(sel_ref[...], om)


def _mla_decode(pt, q2, ckv2, kpe2, cache_c, cache_r, layer, w, *, t_new, pps):
    n_heads, ns, _ = q2.shape
    b2 = ns // t_new
    _, n_pool, page, kv_lora = cache_c.shape
    rope, nope, vdim = w["rope"], w["nope"], w["vdim"]
    n_pages = pt.shape[0] // b2
    nch = n_pages // pps
    assert nch % 2 == 0
    rows = n_heads * t_new
    rq = np.zeros((rows, LANES), np.float32)
    sel = np.zeros((t_new, rows), np.float32)
    for r in range(rows):
        rq[r, np.arange(LANES) % n_heads == r // t_new] = 1.0
        sel[r % t_new, r] = 1.0
    consts = [w["wuk_dh"], jnp.asarray(rq, BF), w["wabs"], w["wuv_flat"], jnp.asarray(sel, BF)]
    kern = functools.partial(_mla_decode_kernel, layer=layer, n_pages=n_pages, pps=pps, nch=nch, t_new=t_new,
                             n_heads=n_heads, nope=nope, rope=rope, vdim=vdim, kblk=4 * page)
    grid_spec = pltpu.PrefetchScalarGridSpec(
        num_scalar_prefetch=1,
        grid=(b2,),
        in_specs=[
            pl.BlockSpec((n_heads, t_new, HEAD_PAD), lambda b, ptr: (0, b, 0)),
            pl.BlockSpec((t_new, kv_lora), lambda b, ptr: (b, 0)),
            pl.BlockSpec((t_new, rope), lambda b, ptr: (b, 0)),
        ] + [pl.BlockSpec(x.shape, lambda b, ptr, nd=x.ndim: (0,) * nd) for x in consts]
        + [pl.BlockSpec(memory_space=pl.ANY)] * 2,
        out_specs=pl.BlockSpec((t_new, n_heads * vdim), lambda b, ptr: (b, 0)),
        scratch_shapes=[
            pltpu.VMEM((2, pps, page, kv_lora), F32), pltpu.VMEM((2, pps, rope, page), F32),
            pltpu.SemaphoreType.DMA((2, 2)),
            pltpu.VMEM((rows, kv_lora), BF), pltpu.VMEM((rows, LANES), BF),
            pltpu.VMEM((rows, 1), F32), pltpu.VMEM((rows, 1), F32), pltpu.VMEM((rows, kv_lora), F32),
            pltpu.VMEM((pps * page, kv_lora), BF), pltpu.VMEM((rope, pps * page), BF),
            pltpu.VMEM((page, LANES), BF),
        ],
    )
    return pl.pallas_call(
        kern, grid_spec=grid_spec,
        out_shape=jax.ShapeDtypeStruct((ns, n_heads * vdim), F32),
        compiler_params=_params(("arbitrary",)),
        name="mla_decode",
    )(pt, q2, ckv2, kpe2, *consts, cache_c, cache_r)


TM = 512
TQ = 256
TKC = 512
PPS = 16


def _ffn_chunk(dff):
    for parts in (2, 1, 4, 11, 22):
        if dff % parts == 0 and (dff // parts) % LANES == 0:
            return dff // parts
    return dff


def kernel(x_prompt, x_sample, cache_fox_k, cache_fox_v, cache_fox_logf, cache_mla_ckv, cache_mla_kpe, page_table, attn_norm, ffn_norm, fox_w_in, fox_b_f, fox_g_q, fox_g_k, fox_w_o, mla_w_in, mla_g_cq, mla_w_uq, mla_g_qn, mla_g_qr, mla_g_ckv, mla_g_kr, mla_w_ukv, mla_g_kn, mla_w_o, ffn_w_gu, ffn_w_down):
    b1, seq, d = x_prompt.shape
    b2, t_new, _ = x_sample.shape
    depth = attn_norm.shape[0]
    n_fox, n_pool, page, fox_kv, fox_hd = cache_fox_k.shape
    fox_heads = fox_b_f.shape[1]
    past = page_table.shape[1] * page
    dff = ffn_w_down.shape[1]
    tf = _ffn_chunk(dff)
    n_p, n_s = b1 * seq, b2 * t_new
    tm_s = min(TM, n_s)
    assert fox_hd == 64 and seq % TKC == 0 and seq % TM == 0 and n_s % tm_s == 0 and tm_s % t_new == 0
    assert page_table.shape[1] % PPS == 0 and page == LANES and t_new == SUBLANES

    hp = x_prompt.reshape(n_p, d)
    hs = x_sample.reshape(n_s, d)
    pt = page_table.reshape(-1).astype(jnp.int32)
    ck4 = jnp.transpose(cache_fox_k, (0, 1, 3, 4, 2)).reshape(n_fox, n_pool, fox_kv * fox_hd, page)
    cv4 = jnp.transpose(cache_fox_v, (0, 1, 3, 4, 2)).reshape(n_fox, n_pool, fox_kv * fox_hd, page)
    clf = jnp.transpose(cache_fox_logf, (0, 1, 3, 2))
    ckr = jnp.transpose(cache_mla_kpe, (0, 1, 3, 2))

    tri_p = jnp.asarray(np.tril(np.ones((TM, TM), np.float32)), BF)
    seq_id = np.arange(tm_s) // t_new
    tri_s = jnp.asarray(np.tril(np.ones((tm_s, tm_s), np.float32)) * (seq_id[:, None] == seq_id[None, :]), BF)
    rope_half = mla_g_kr.shape[1] // 2
    cos_p, sin_p = _rope_tables(jnp.arange(seq), rope_half)
    cos_s, sin_s = _rope_tables(past + jnp.arange(tm_s) % t_new, rope_half)

    fk_p, fv_p, fl_p, fk_s, fv_s, fl_s = [], [], [], [], [], []
    mc_p, mr_p, mc_s, mr_s = [], [], [], []
    for i in range(depth):
        j = i // 2
        if i % 2 == 0:
            w = _fox_weights(fox_w_in[j], fox_b_f[j], fox_g_q[j], fox_g_k[j], attn_norm[i], fox_heads, fox_kv, fox_hd)
            q, ka, vp, k, v, lf, _ = _fox_in(hp, w, tm=TM, tiles_per_seq=seq // TM, tri=tri_p)
            o_p = _flash_prompt(q, ka, vp, batch=b1, n_q=fox_heads // fox_kv, n_k=1, tq=TQ, tkc=TKC, vdim=fox_hd)
            q2, _, _, k2, v2, lf2, c2 = _fox_in(hs, w, tm=tm_s, tiles_per_seq=1, tri=tri_s)
            o_s = _fox_decode(pt, q2, c2, k2, v2, ck4, cv4, clf, j, t_new=t_new, pps=PPS)
            fk_p.append(k); fv_p.append(v); fl_p.append(lf)
            fk_s.append(k2); fv_s.append(v2); fl_s.append(lf2)
            wo = fox_w_o[j].astype(BF)
        else:
            w = _mla_weights(mla_w_in[j], mla_g_cq[j], mla_w_uq[j], mla_g_qn[j], mla_g_qr[j], mla_g_ckv[j],
                             mla_g_kr[j], mla_w_ukv[j], mla_g_kn[j], attn_norm[i])
            q, kk, vv, ckv, kpe = _mla_in(hp, cos_p, sin_p, w, tm=TM)
            o_p = _flash_prompt(q, kk, vv, batch=b1, n_q=2, n_k=2, tq=TQ, tkc=TKC, vdim=w["vdim"])
            q2, _, _, ckv2, kpe2 = _mla_in(hs, cos_s, sin_s, w, tm=tm_s)
            o_s = _mla_decode(pt, q2, ckv2, kpe2, cache_mla_ckv, ckr, j, w, t_new=t_new, pps=PPS)
            mc_p.append(ckv); mr_p.append(kpe)
            mc_s.append(ckv2); mr_s.append(kpe2)
            wo = mla_w_o[j].astype(BF)
        gf = ffn_norm[i].reshape(1, d)
        wg = ffn_w_gu[i, :, :dff].astype(BF)
        wu = ffn_w_gu[i, :, dff:].astype(BF)
        wd = ffn_w_down[i].astype(BF)
        hp = _out_ffn(hp, o_p, wo, gf, wg, wu, wd, tm=TM, tf=tf)
        hs = _out_ffn(hs, o_s, wo, gf, wg, wu, wd, tm=tm_s, tf=tf)

    def stk(xs, lead, tail):
        return jnp.stack(xs).reshape((len(xs),) + lead + tail)

    return (hp.reshape(b1, seq, d), hs.reshape(b2, t_new, d),
            stk(fk_p, (b1, seq), (fox_kv, fox_hd)), stk(fv_p, (b1, seq), (fox_kv, fox_hd)),
            stk(fl_p, (b1, seq), (fox_heads,)),
            stk(mc_p, (b1, seq), (mla_g_ckv.shape[1],)), stk(mr_p, (b1, seq), (mla_g_kr.shape[1],)),
            stk(fk_s, (b2, t_new), (fox_kv, fox_hd)), stk(fv_s, (b2, t_new), (fox_kv, fox_hd)),
            stk(fl_s, (b2, t_new), (fox_heads,)),
            stk(mc_s, (b2, t_new), (mla_g_ckv.shape[1],)), stk(mr_s, (b2, t_new), (mla_g_kr.shape[1],)))
```

```python
import functools

import numpy as np
import jax
import jax.numpy as jnp
from jax import lax
from jax.experimental import pallas as pl
from jax.experimental.pallas import tpu as pltpu

BF = jnp.bfloat16
F32 = jnp.float32
EPS = 1e-6
ROPE_THETA = 10000.0
LANES = 128
SUBLANES = 8
HEAD_PAD = LANES
NEG = -1e30
LOG2E = 1.4426950408889634
VMEM_LIMIT = 56 * 1024 * 1024
NT = (((1,), (1,)), ((), ()))


def _dot(a, b):
    return jnp.dot(a, b, preferred_element_type=F32)


def _dot_nt(a, b):
    return lax.dot_general(a, b, NT, preferred_element_type=F32)


def _split3(x):
    hi = x.astype(BF)
    r1 = x - hi.astype(F32)
    mid = r1.astype(BF)
    lo = (r1 - mid.astype(F32)).astype(BF)
    return hi, mid, lo


def _dot_exact_lhs(x, w):
    hi, mid, lo = _split3(x)
    return _dot(hi, w) + _dot(mid, w) + _dot(lo, w)


def _dot_exact_rhs(w, x):
    hi, mid, lo = _split3(x)
    return _dot(w, hi) + _dot(w, mid) + _dot(w, lo)


def _rms_rows(x, g):
    return x * lax.rsqrt(jnp.mean(x * x, axis=-1, keepdims=True) + EPS) * g


def _log_sigmoid(x):
    return jnp.minimum(x, 0.0) - jnp.log1p(jnp.exp(-jnp.abs(x)))


def _full(shape):
    n = len(shape)
    return pl.BlockSpec(shape, lambda *_: (0,) * n)


def _params(sem):
    return pltpu.CompilerParams(dimension_semantics=sem, vmem_limit_bytes=VMEM_LIMIT)


def _fox_in_kernel(x_ref, ga_ref, wq_ref, wk_ref, wv_ref, wkv_ref, wf_ref, bf_ref, gq_ref, gk_ref, gk2_ref,
                   g1_ref, g2_ref, tri_ref, pq_ref, pk_ref, oq_ref, ok_ref,
                   q_out, ka_out, vp_out, k_out, v_out, lf_out, c_out, carry_ref,
                   *, tiles_per_seq, n_heads, n_kv, n_gate, hd):
    i = pl.program_id(0)
    tm = x_ref.shape[0]
    xn = _rms_rows(x_ref[...], ga_ref[...]).astype(BF)
    lane = lax.broadcasted_iota(jnp.int32, (tm, LANES), 1)

    lf = jnp.where(lane < n_gate, _log_sigmoid(_dot(xn, wf_ref[...]) + bf_ref[...]), 0.0)
    lf_out[...] = lf[:, :n_gate]

    @pl.when(i % tiles_per_seq == 0)
    def _():
        carry_ref[...] = jnp.zeros_like(carry_ref)

    c = _dot_exact_rhs(tri_ref[...], lf) + carry_ref[...]
    carry_ref[...] = c[tm - 1:tm, :]
    c = c * LOG2E
    c_out[...] = c

    ch, cm, cl = _split3(c)
    cparts = (ch.astype(F32) + pltpu.roll(cm.astype(F32), 16, 1) + pltpu.roll(cl.astype(F32), 32, 1)).astype(BF)

    g1 = g1_ref[...]
    v_ones = jnp.where(lane >= hd, 1.0, 0.0)
    for j in range(n_heads // 2):
        sl = slice(2 * HEAD_PAD * j, 2 * HEAD_PAD * (j + 1))
        zq = _dot(xn, wq_ref[:, sl])
        aug = _dot(cparts, pq_ref[:, sl]) + oq_ref[:, sl]
        for u in range(2):
            z = zq[:, HEAD_PAD * u:HEAD_PAD * (u + 1)]
            ms = _dot((z * z).astype(BF), g1)
            qn = z * lax.rsqrt(ms + EPS) * gq_ref[...]
            q_out[2 * j + u] = (qn + aug[:, HEAD_PAD * u:HEAD_PAD * (u + 1)]).astype(BF)

    for j in range(n_kv // 2):
        sl = slice(2 * HEAD_PAD * j, 2 * HEAD_PAD * (j + 1))
        zk = _dot(xn, wk_ref[:, sl])
        zv = _dot(xn, wv_ref[:, sl])
        aug = _dot(cparts, pk_ref[:, sl]) + ok_ref[:, sl]
        for u in range(2):
            z = zk[:, HEAD_PAD * u:HEAD_PAD * (u + 1)]
            ms = _dot((z * z).astype(BF), g1)
            kn = z * lax.rsqrt(ms + EPS) * gk_ref[...]
            ka_out[2 * j + u] = (kn + aug[:, HEAD_PAD * u:HEAD_PAD * (u + 1)]).astype(BF)
            vp_out[2 * j + u] = (zv[:, HEAD_PAD * u:HEAD_PAD * (u + 1)] + v_ones).astype(BF)

    zkv = _dot(xn, wkv_ref[...])
    nk = wkv_ref.shape[1] // 2
    k = zkv[:, :nk]
    ms = _dot((k * k).astype(BF), g2_ref[...])
    k_out[...] = k * lax.rsqrt(ms + EPS) * gk2_ref[...]
    v_out[...] = zkv[:, nk:]


def _fox_in(x, w, *, tm, tiles_per_seq, tri):
    n, d = x.shape
    n_heads, n_kv, n_gate, hd = w["n_heads"], w["n_kv"], w["n_gate"], w["head_dim"]
    kern = functools.partial(_fox_in_kernel, tiles_per_seq=tiles_per_seq, n_heads=n_heads, n_kv=n_kv, n_gate=n_gate,
                             hd=hd)
    consts = [w["ga"], w["wq"], w["wk"], w["wv"], w["wkv"], w["wf"], w["bf"], w["gq"], w["gk"], w["gk2"],
              w["g1"], w["g2"], tri, w["pq"], w["pk"], w["oq"], w["ok"]]
    return pl.pallas_call(
        kern,
        grid=(n // tm,),
        in_specs=[pl.BlockSpec((tm, d), lambda i: (i, 0))] + [_full(c.shape) for c in consts],
        out_specs=[
            pl.BlockSpec((n_heads, tm, HEAD_PAD), lambda i: (0, i, 0)),
            pl.BlockSpec((n_kv, tm, HEAD_PAD), lambda i: (0, i, 0)),
            pl.BlockSpec((n_kv, tm, HEAD_PAD), lambda i: (0, i, 0)),
            pl.BlockSpec((tm, n_kv * hd), lambda i: (i, 0)),
            pl.BlockSpec((tm, n_kv * hd), lambda i: (i, 0)),
            pl.BlockSpec((tm, n_gate), lambda i: (i, 0)),
            pl.BlockSpec((tm, LANES), lambda i: (i, 0)),
        ],
        out_shape=[
            jax.ShapeDtypeStruct((n_heads, n, HEAD_PAD), BF),
            jax.ShapeDtypeStruct((n_kv, n, HEAD_PAD), BF),
            jax.ShapeDtypeStruct((n_kv, n, HEAD_PAD), BF),
            jax.ShapeDtypeStruct((n, n_kv * hd), F32),
            jax.ShapeDtypeStruct((n, n_kv * hd), F32),
            jax.ShapeDtypeStruct((n, n_gate), F32),
            jax.ShapeDtypeStruct((n, LANES), F32),
        ],
        scratch_shapes=[pltpu.VMEM((1, LANES), F32)],
        compiler_params=_params(("arbitrary",)),
        name="fox_in",
    )(x, *consts)


def _fox_weights(w_in, b_f, g_q, g_k, g_attn, n_heads, n_kv, hd):
    d = w_in.shape[0]
    nq, nk = n_heads * hd, n_kv * hd
    n_gate = n_heads
    grp = n_heads // n_kv
    scale = hd ** -0.5
    pad = HEAD_PAD - hd

    def heads_padded(wm, nh):
        return jnp.pad(wm.reshape(d, nh, hd), ((0, 0), (0, 0), (0, pad))).reshape(d, nh * HEAD_PAD).astype(BF)

    ck0, cq0 = hd, hd + 3 * grp
    pq = np.zeros((LANES, n_heads * HEAD_PAD), np.float32)
    oq = np.zeros((1, n_heads * HEAD_PAD), np.float32)
    pk = np.zeros((LANES, n_kv * HEAD_PAD), np.float32)
    ok = np.zeros((1, n_kv * HEAD_PAD), np.float32)
    for h in range(n_heads):
        kv, g = divmod(h, grp)
        for p in range(3):
            pq[16 * p + h, HEAD_PAD * h + cq0 + p] = 1.0
            oq[0, HEAD_PAD * h + ck0 + 3 * g + p] = 1.0
            pk[16 * p + h, HEAD_PAD * kv + ck0 + 3 * g + p] = -1.0
    for kv in range(n_kv):
        for p in range(3):
            ok[0, HEAD_PAD * kv + cq0 + p] = 1.0
    g1 = np.zeros((HEAD_PAD, HEAD_PAD), np.float32)
    g1[:hd, :hd] = 1.0 / hd
    g2 = np.kron(np.eye(n_kv, dtype=np.float32), np.full((hd, hd), 1.0 / hd, np.float32))
    return dict(
        n_heads=n_heads, n_kv=n_kv, n_gate=n_gate, head_dim=hd,
        ga=g_attn.reshape(1, d),
        wq=heads_padded(w_in[:, :nq], n_heads),
        wk=heads_padded(w_in[:, nq:nq + nk], n_kv),
        wv=heads_padded(w_in[:, nq + nk:nq + 2 * nk], n_kv),
        wkv=w_in[:, nq:nq + 2 * nk].astype(BF),
        wf=jnp.pad(w_in[:, nq + 2 * nk:], ((0, 0), (0, LANES - n_gate))).astype(BF),
        bf=jnp.pad(b_f, (0, LANES - n_gate)).reshape(1, LANES),
        gq=jnp.pad(g_q * (scale * LOG2E), (0, pad)).reshape(1, HEAD_PAD),
        gk=jnp.pad(g_k, (0, pad)).reshape(1, HEAD_PAD),
        gk2=jnp.tile(g_k, n_kv).reshape(1, nk),
        g1=jnp.asarray(g1, BF), g2=jnp.asarray(g2, BF),
        pq=jnp.asarray(pq, BF), pk=jnp.asarray(pk, BF), oq=jnp.asarray(oq), ok=jnp.asarray(ok),
    )


def _rope_lanes(x, cos, sin, lane, lo, half):
    nl = x.shape[-1]
    fwd = pltpu.roll(x, nl - half, 1)
    bwd = pltpu.roll(x, half, 1)
    first = (lane >= lo) & (lane < lo + half)
    second = (lane >= lo + half) & (lane < lo + 2 * half)
    rot = jnp.where(first, -fwd, jnp.where(second, bwd, 0.0))
    return jnp.where(first | second, x * cos + rot * sin, x)


def _mla_in_kernel(x_ref, cos_ref, sin_ref, ga_ref, win_ref, gcq_ref, gckv_ref, gkr_ref, wuq_ref, gq_ref,
                   wuk_ref, gk_ref, wuv_ref, g1_ref,
                   q_out, kk_out, vv_out, ckv_out, kpe_out,
                   *, n_heads, q_lora, kv_lora, nope, rope, vdim):
    tm = x_ref.shape[0]
    lane = lax.broadcasted_iota(jnp.int32, (tm, LANES), 1)
    cos, sin = cos_ref[...], sin_ref[...]
    xn = _rms_rows(x_ref[...], ga_ref[...]).astype(BF)
    z = _dot(xn, win_ref[...])
    cq = _rms_rows(z[:, :q_lora], gcq_ref[...]).astype(BF)
    ckv = _rms_rows(z[:, q_lora:q_lora + kv_lora], gckv_ref[...])
    ckv_out[...] = ckv
    ckv_b = ckv.astype(BF)
    kr = z[:, q_lora + kv_lora:]
    kr = kr * lax.rsqrt(jnp.sum(kr * kr, axis=-1, keepdims=True) * (1.0 / rope) + EPS) * gkr_ref[...]
    kpe = _rope_lanes(kr, cos, sin, lane, 0, rope // 2)
    kpe_out[...] = kpe[:, :rope]
    kpe_sh = pltpu.roll(kpe, nope, 1)

    g1 = g1_ref[...]
    v_ones = jnp.where(lane >= vdim, 1.0, 0.0)
    for j in range(n_heads // 2):
        sl = slice(2 * HEAD_PAD * j, 2 * HEAD_PAD * (j + 1))
        zq = _dot(cq, wuq_ref[:, sl])
        zk = _dot(ckv_b, wuk_ref[:, sl])
        zv = _dot(ckv_b, wuv_ref[:, sl])
        for u in range(2):
            bl = slice(HEAD_PAD * u, HEAD_PAD * (u + 1))
            q = zq[:, bl]
            q = q * lax.rsqrt(_dot((q * q).astype(BF), g1) + EPS) * gq_ref[...]
            q_out[2 * j + u] = _rope_lanes(q, cos, sin, lane, nope, rope // 2).astype(BF)
            k = zk[:, bl]
            k = k * lax.rsqrt(_dot((k * k).astype(BF), g1) + EPS) * gk_ref[...]
            kk_out[2 * j + u] = (k + kpe_sh).astype(BF)
            vv_out[2 * j + u] = (zv[:, bl] + v_ones).astype(BF)


def _mla_in(x, cos, sin, w, *, tm):
    n, d = x.shape
    n_heads, kv_lora, rope = w["n_heads"], w["kv_lora"], w["rope"]
    tbl_tiles = cos.shape[0] // tm
    kern = functools.partial(_mla_in_kernel, n_heads=n_heads, q_lora=w["q_lora"], kv_lora=kv_lora,
                             nope=w["nope"], rope=rope, vdim=w["vdim"])
    consts = [w["ga"], w["win"], w["gcq"], w["gckv"], w["gkr"], w["wuq"], w["gq"], w["wuk"], w["gk"], w["wuv"], w["g1"]]
    head_spec = pl.BlockSpec((n_heads, tm, HEAD_PAD), lambda i: (0, i, 0))
    head_shape = jax.ShapeDtypeStruct((n_heads, n, HEAD_PAD), BF)
    return pl.pallas_call(
        kern,
        grid=(n // tm,),
        in_specs=[pl.BlockSpec((tm, d), lambda i: (i, 0)),
                  pl.BlockSpec((tm, LANES), lambda i: (i % tbl_tiles, 0)),
                  pl.BlockSpec((tm, LANES), lambda i: (i % tbl_tiles, 0))] + [_full(c.shape) for c in consts],
        out_specs=[head_spec, head_spec, head_spec,
                   pl.BlockSpec((tm, kv_lora), lambda i: (i, 0)),
                   pl.BlockSpec((tm, rope), lambda i: (i, 0))],
        out_shape=[head_shape, head_shape, head_shape,
                   jax.ShapeDtypeStruct((n, kv_lora), F32),
                   jax.ShapeDtypeStruct((n, rope), F32)],
        compiler_params=_params(("parallel",)),
        name="mla_in",
    )(x, cos, sin, *consts)


def _mla_weights(w_in, g_cq, w_uq, g_qn, g_qr, g_ckv, g_kr, w_ukv, g_kn, g_attn):
    d = w_in.shape[0]
    q_lora, kv_lora, rope, nope = g_cq.shape[0], g_ckv.shape[0], g_kr.shape[0], g_qn.shape[0]
    n_heads = w_uq.shape[1]
    vdim = w_ukv.shape[2] - nope
    scale = (nope + rope) ** -0.5
    in_pad = -w_in.shape[1] % LANES
    w_uk, w_uv = w_ukv[..., :nope], w_ukv[..., nope:]

    def heads_padded(wm):
        r, nh, hd = wm.shape
        return jnp.pad(wm, ((0, 0), (0, 0), (0, HEAD_PAD - hd))).reshape(r, nh * HEAD_PAD).astype(BF)

    g1 = np.zeros((HEAD_PAD, HEAD_PAD), np.float32)
    g1[:nope, :nope] = 1.0 / nope
    g1[nope:nope + rope, nope:nope + rope] = 1.0 / rope
    wuk_dh = jnp.transpose(w_uk, (0, 2, 1)).reshape(kv_lora, nope * n_heads).astype(BF)
    wabs = jnp.transpose(w_uk * g_kn[None, None, :], (1, 2, 0))
    wabs = jnp.pad(wabs, ((0, 0), (0, HEAD_PAD - nope), (0, 0))).reshape(n_heads * HEAD_PAD, kv_lora).astype(BF)
    return dict(
        n_heads=n_heads, q_lora=q_lora, kv_lora=kv_lora, rope=rope, nope=nope, vdim=vdim,
        ga=g_attn.reshape(1, d),
        win=jnp.pad(w_in, ((0, 0), (0, in_pad))).astype(BF),
        gcq=g_cq.reshape(1, -1), gckv=g_ckv.reshape(1, -1),
        gkr=jnp.pad(g_kr, (0, LANES - rope)).reshape(1, LANES),
        wuq=heads_padded(w_uq),
        gq=jnp.pad(jnp.concatenate([g_qn, g_qr]) * (scale * LOG2E), (0, HEAD_PAD - nope - rope)).reshape(1, HEAD_PAD),
        wuk=heads_padded(w_uk), gk=jnp.pad(g_kn, (0, HEAD_PAD - nope)).reshape(1, HEAD_PAD),
        wuv=heads_padded(w_uv), g1=jnp.asarray(g1, BF),
        wuk_dh=wuk_dh, wabs=wabs, wuv_flat=w_uv.reshape(kv_lora, n_heads * vdim).astype(BF),
    )


def _rope_tables(pos, half):
    inv = ROPE_THETA ** (-jnp.arange(half, dtype=F32) / half)
    ang = pos.astype(F32)[:, None] * inv[None, :]
    reps = LANES // half
    return jnp.tile(jnp.cos(ang), (1, reps)), jnp.tile(jnp.sin(ang), (1, reps))


def _flash_kernel(q_ref, k_ref, v_ref, o_ref, s_ref, m_ref, acc_ref, *, n_q, n_k, vdim, tkc):
    qi = pl.program_id(2)
    tq = q_ref.shape[1]
    diag = (qi * tq) // tkc
    off = qi * tq - diag * tkc
    lane_tiles = tkc // LANES

    m_ref[...] = jnp.full_like(m_ref, NEG)
    acc_ref[...] = jnp.zeros_like(acc_ref)

    def lane_max(s):
        m = s[:, :LANES]
        for t in range(1, lane_tiles):
            m = jnp.maximum(m, s[:, LANES * t:LANES * (t + 1)])
        return m

    def score_chunk(kc, keep):
        k0 = pl.multiple_of(kc * tkc, tkc)
        for r in range(n_q):
            s = _dot_nt(q_ref[r], k_ref[r * n_k // n_q, pl.ds(k0, tkc), :])
            if keep is not None:
                s = jnp.where(keep, s, NEG)
            s_ref[r, kc] = s
            m_ref[r] = jnp.maximum(m_ref[r], lane_max(s))

    def pass_a(kc, carry):
        score_chunk(kc, None)
        return carry

    lax.fori_loop(0, diag, pass_a, 0)
    row = lax.broadcasted_iota(jnp.int32, (tq, tkc), 0)
    col = lax.broadcasted_iota(jnp.int32, (tq, tkc), 1)
    score_chunk(diag, col <= row + off)

    for r in range(n_q):
        m_ref[r] = jnp.broadcast_to(jnp.max(m_ref[r], axis=-1, keepdims=True), (tq, LANES))

    def pass_b(kc, carry):
        k0 = pl.multiple_of(kc * tkc, tkc)
        for r in range(n_q):
            m = m_ref[r]
            p = jnp.exp2(s_ref[r, kc] - jnp.concatenate([m] * lane_tiles, axis=1)).astype(BF)
            acc_ref[r] += _dot(p, v_ref[r * n_k // n_q, pl.ds(k0, tkc), :])
        return carry

    lax.fori_loop(0, diag + 1, pass_b, 0)

    lane = lax.broadcasted_iota(jnp.int32, (tq, LANES), 1)

    def normalised(r):
        a = acc_ref[r]
        return jnp.where(lane < vdim, a / jnp.where(lane < vdim, pltpu.roll(a, vdim, 1), 1.0), 0.0)

    for j in range(n_q // 2):
        o_ref[:, LANES * j:LANES * (j + 1)] = (
            normalised(2 * j) + pltpu.roll(normalised(2 * j + 1), vdim, 1)).astype(o_ref.dtype)


def _flash_prompt(q, k, v, *, batch, n_q, n_k, tq, tkc, vdim):
    hq, n, _ = q.shape
    seq = n // batch
    nb = seq // tq
    assert tkc % tq == 0 and seq % tkc == 0 and 2 * vdim == LANES
    kern = functools.partial(_flash_kernel, n_q=n_q, n_k=n_k, vdim=vdim, tkc=tkc)
    return pl.pallas_call(
        kern,
        grid=(batch, hq // n_q, nb),
        in_specs=[
            pl.BlockSpec((n_q, tq, HEAD_PAD), lambda b, j, i: (j, b * nb + i, 0)),
            pl.BlockSpec((n_k, seq, HEAD_PAD), lambda b, j, i: (j, b, 0)),
            pl.BlockSpec((n_k, seq, HEAD_PAD), lambda b, j, i: (j, b, 0)),
        ],
        out_specs=pl.BlockSpec((tq, n_q * vdim), lambda b, j, i: (b * nb + i, j)),
        out_shape=jax.ShapeDtypeStruct((n, hq * vdim), BF),
        scratch_shapes=[pltpu.VMEM((n_q, seq // tkc, tq, tkc), F32), pltpu.VMEM((n_q, tq, LANES), F32),
                        pltpu.VMEM((n_q, tq, HEAD_PAD), F32)],
        compiler_params=_params(("parallel", "parallel", "arbitrary")),
        name="flash_prompt",
    )(q, k, v)


def _out_ffn_kernel(h_ref, o_ref, wo_ref, gf_ref, wg_ref, wu_ref, wd_ref, out_ref, h1_ref, xn_ref, acc_ref):
    f = pl.program_id(1)

    @pl.when(f == 0)
    def _():
        h1 = h_ref[...] + _dot(o_ref[...].astype(BF), wo_ref[...])
        h1_ref[...] = h1
        xn_ref[...] = _rms_rows(h1, gf_ref[...]).astype(BF)
        acc_ref[...] = jnp.zeros_like(acc_ref)

    xn = xn_ref[...]
    g = _dot(xn, wg_ref[...])
    u = _dot(xn, wu_ref[...])
    acc_ref[...] += _dot((g * jax.nn.sigmoid(g) * u).astype(BF), wd_ref[...])

    @pl.when(f == pl.num_programs(1) - 1)
    def _():
        out_ref[...] = h1_ref[...] + acc_ref[...]


def _out_ffn(h, o, wo, gf, wg, wu, wd, *, tm, tf):
    n, d = h.shape
    do = o.shape[1]
    dff = wg.shape[1]
    return pl.pallas_call(
        _out_ffn_kernel,
        grid=(n // tm, dff // tf),
        in_specs=[
            pl.BlockSpec((tm, d), lambda i, f: (i, 0)),
            pl.BlockSpec((tm, do), lambda i, f: (i, 0)),
            pl.BlockSpec((do, d), lambda i, f: (0, 0)),
            pl.BlockSpec((1, d), lambda i, f: (0, 0)),
            pl.BlockSpec((d, tf), lambda i, f: (0, f)),
            pl.BlockSpec((d, tf), lambda i, f: (0, f)),
            pl.BlockSpec((tf, d), lambda i, f: (f, 0)),
        ],
        out_specs=pl.BlockSpec((tm, d), lambda i, f: (i, 0)),
        out_shape=jax.ShapeDtypeStruct((n, d), F32),
        scratch_shapes=[pltpu.VMEM((tm, d), F32), pltpu.VMEM((tm, d), BF), pltpu.VMEM((tm, d), F32)],
        compiler_params=_params(("parallel", "arbitrary")),
        name="out_ffn",
    )(h, o, wo, gf, wg, wu, wd)


def _softmax_step(s, v, m_ref, l_ref, acc_ref, v_is_transposed=False):
    m_prev = m_ref[...]
    m_new = jnp.maximum(m_prev, jnp.max(s, axis=-1, keepdims=True))
    alpha = jnp.exp2(m_prev - m_new)
    p = jnp.exp2(s - m_new)
    l_ref[...] = alpha * l_ref[...] + jnp.sum(p, axis=-1, keepdims=True)
    p = p.astype(BF)
    acc_ref[...] = alpha * acc_ref[...] + (_dot_nt(p, v) if v_is_transposed else _dot(p, v))
    m_ref[...] = m_new


def _rows_from_heads(x, t_new):
    return jnp.concatenate([jnp.broadcast_to(x[h:h + 1], (t_new, x.shape[1])) for h in range(x.shape[0])], axis=0)


def _chunk_pipeline(pt_ref, srcs, bufs, sems, *, layer, n_pages, pps, nch, reverse, compute):
    b = pl.program_id(0)
    n_seq = pl.num_programs(0)

    def chunk_copies(seq, c, slot):
        first = ((nch - 1 - c) if reverse else c) * pps
        out = []
        for p in range(pps):
            pid = pt_ref[seq * n_pages + first + p]
            for i, (src, buf) in enumerate(zip(srcs, bufs)):
                out.append(pltpu.make_async_copy(src.at[layer, pid], buf.at[slot, p], sems.at[i, slot]))
        return out

    @pl.when(b == 0)
    def _():
        for cp in chunk_copies(b, 0, 0):
            cp.start()

    for c in range(nch):
        slot = c % 2
        if c + 1 < nch:
            for cp in chunk_copies(b, c + 1, 1 - slot):
                cp.start()
        else:
            @pl.when(b + 1 < n_seq)
            def _():
                for cp in chunk_copies(b + 1, 0, 1 - slot):
                    cp.start()
        for cp in chunk_copies(b, c, slot):
            cp.wait()
        compute(c, slot)


def _fox_decode_kernel(pt_ref, q_ref, c_ref, k2_ref, v2_ref, u_ref, pg_ref, sg_ref, ck_hbm, cv_hbm, cl_hbm,
                       o_ref, kbuf, vbuf, lbuf, sems, qbd_ref, m_ref, l_ref, acc_ref, kc_ref, vc_ref,
                       *, layer, n_pages, pps, nch, t_new, n_heads, n_kv, hd):
    rows = n_heads * t_new
    page = kbuf.shape[3]
    kvw = n_kv * hd
    row = lax.broadcasted_iota(jnp.int32, (rows, LANES), 0)
    lane = lax.broadcasted_iota(jnp.int32, (rows, LANES), 1)

    a = q_ref[...].astype(F32).reshape(rows, HEAD_PAD)
    a = jnp.where(lane < hd, a, 0.0)
    kvh = row // (rows // n_kv)
    for blk in range(kvw // LANES):
        parts = 0.0
        for u in range(LANES // hd):
            parts = parts + jnp.where(kvh == blk * (LANES // hd) + u, pltpu.roll(a, hd * u, 1) if u else a, 0.0)
        qbd_ref[:, LANES * blk:LANES * (blk + 1)] = parts.astype(BF)
    cn = c_ref[...]
    cn_rows = jnp.concatenate([cn] * n_heads, axis=0)
    cn_row = jnp.sum(jnp.where(lane == row // t_new, cn_rows, 0.0), axis=-1, keepdims=True)
    m_ref[...] = jnp.full_like(m_ref, NEG)
    l_ref[...] = jnp.zeros_like(l_ref)
    acc_ref[...] = jnp.zeros_like(acc_ref)
    zpad = jnp.zeros((page - t_new, kvw), F32)
    kn = jnp.concatenate([k2_ref[...], zpad], axis=0).astype(BF)
    vn = jnp.concatenate([v2_ref[...], zpad], axis=0).astype(BF)
    cn_pad = jnp.concatenate([cn, jnp.zeros((page - t_new, LANES), F32)], axis=0)
    cn_t = cn_pad.T[:n_heads]
    bias = cn_row - _rows_from_heads(cn_t, t_new)
    keep = (lane < t_new) & (lane <= row % t_new)
    s = jnp.where(keep, _dot_nt(qbd_ref[...], kn) + bias, NEG)
    _softmax_step(s, vn, m_ref, l_ref, acc_ref)

    u_mat = u_ref[...]
    carry = [jnp.zeros((n_heads, 1), F32)]

    def compute(c, slot):
        sfx = [None] * pps
        for p in reversed(range(pps)):
            lf_t = lbuf[slot, p]
            sfx[p] = _dot_exact_lhs(lf_t, u_mat) + carry[0]
            carry[0] = carry[0] + jnp.sum(lf_t, axis=-1, keepdims=True)
            kc_ref[:, page * p:page * (p + 1)] = kbuf[slot, p].astype(BF)
            vc_ref[:, page * p:page * (p + 1)] = vbuf[slot, p].astype(BF)
        bias = _rows_from_heads(jnp.concatenate(sfx, axis=1) * LOG2E, t_new) + cn_row
        s = _dot(qbd_ref[...], kc_ref[...]) + bias
        _softmax_step(s, vc_ref[...], m_ref, l_ref, acc_ref, v_is_transposed=True)

    _chunk_pipeline(pt_ref, (ck_hbm, cv_hbm, cl_hbm), (kbuf, vbuf, lbuf), sems, layer=layer, n_pages=n_pages,
                    pps=pps, nch=nch, reverse=True, compute=compute)

    o = acc_ref[...] / l_ref[...]
    row2 = lax.broadcasted_iota(jnp.int32, (rows, kvw), 0)
    lane2 = lax.broadcasted_iota(jnp.int32, (rows, kvw), 1)
    om = jnp.where(lane2 // hd == row2 // (rows // n_kv), o, 0.0).astype(BF)
    out = jnp.zeros(o_ref.shape, F32)
    for g in range(n_heads // n_kv):
        out = out + _dot(sg_ref[g], _dot(om, pg_ref[g]).astype(BF))
    o_ref[...] = out


def _fox_decode(pt, q2, c2, k2, v2, cache_k, cache_v, cache_lf, layer, *, t_new, pps):
    n_heads, ns, _ = q2.shape
    b2 = ns // t_new
    _, n_pool, kvw, page = cache_k.shape
    n_gate = cache_lf.shape[2]
    n_pages = pt.shape[0] // b2
    nch = n_pages // pps
    assert nch % 2 == 0
    hd = 64
    n_kv = kvw // hd
    grp = n_heads // n_kv
    rows = n_heads * t_new
    u_mat = jnp.asarray(np.tril(np.ones((page, page), np.float32), -1), BF)
    pg = np.zeros((grp, kvw, n_heads * hd), np.float32)
    sg = np.zeros((grp, t_new, rows), np.float32)
    for h in range(n_heads):
        kv, g = divmod(h, grp)
        for dd in range(hd):
            pg[g, kv * hd + dd, h * hd + dd] = 1.0
        for t in range(t_new):
            sg[g, t, h * t_new + t] = 1.0
    consts = [u_mat, jnp.asarray(pg, BF), jnp.asarray(sg, BF)]
    kern = functools.partial(_fox_decode_kernel, layer=layer, n_pages=n_pages, pps=pps, nch=nch, t_new=t_new,
                             n_heads=n_heads, n_kv=n_kv, hd=hd)
    grid_spec = pltpu.PrefetchScalarGridSpec(
        num_scalar_prefetch=1,
        grid=(b2,),
        in_specs=[
            pl.BlockSpec((n_heads, t_new, HEAD_PAD), lambda b, ptr: (0, b, 0)),
            pl.BlockSpec((t_new, LANES), lambda b, ptr: (b, 0)),
            pl.BlockSpec((t_new, kvw), lambda b, ptr: (b, 0)),
            pl.BlockSpec((t_new, kvw), lambda b, ptr: (b, 0)),
        ] + [pl.BlockSpec(x.shape, lambda b, ptr, nd=x.ndim: (0,) * nd) for x in consts]
        + [pl.BlockSpec(memory_space=pl.ANY)] * 3,
        out_specs=pl.BlockSpec((t_new, n_heads * hd), lambda b, ptr: (b, 0)),
        scratch_shapes=[
            pltpu.VMEM((2, pps, kvw, page), F32), pltpu.VMEM((2, pps, kvw, page), F32),
            pltpu.VMEM((2, pps, n_gate, page), F32), pltpu.SemaphoreType.DMA((3, 2)),
            pltpu.VMEM((rows, kvw), BF),
            pltpu.VMEM((rows, 1), F32), pltpu.VMEM((rows, 1), F32), pltpu.VMEM((rows, kvw), F32),
            pltpu.VMEM((kvw, pps * page), BF), pltpu.VMEM((kvw, pps * page), BF),
        ],
    )
    return pl.pallas_call(
        kern, grid_spec=grid_spec,
        out_shape=jax.ShapeDtypeStruct((ns, n_heads * hd), F32),
        compiler_params=_params(("arbitrary",)),
        name="fox_decode",
    )(pt, q2, c2, k2, v2, *consts, cache_k, cache_v, cache_lf)


def _mla_decode_kernel(pt_ref, q_ref, c2_ref, r2_ref, wdh_ref, rq_ref, wabs_ref, wuv_ref, sel_ref, cc_hbm, cr_hbm,
                       o_ref, cbuf, rbuf, sems, qa_ref, qr_ref, m_ref, l_ref, acc_ref, cc_ref, rc_ref, rn_ref,
                       *, layer, n_pages, pps, nch, t_new, n_heads, nope, rope, vdim, kblk):
    rows = n_heads * t_new
    page = cbuf.shape[2]
    row = lax.broadcasted_iota(jnp.int32, (rows, LANES), 0)
    lane = lax.broadcasted_iota(jnp.int32, (rows, LANES), 1)

    def nope_scores(n_keys):
        blk = min(kblk, n_keys)
        inv = []
        for kb in range(n_keys // blk):
            kf = _dot(cc_ref[kb * blk:(kb + 1) * blk, :], wdh_ref[...])
            sq = kf * kf
            part = sq[:, :LANES]
            for t in range(1, sq.shape[1] // LANES):
                part = part + sq[:, LANES * t:LANES * (t + 1)]
            ssq = _dot_nt(rq_ref[...], part.astype(BF))
            inv.append(lax.rsqrt(ssq * (1.0 / nope) + EPS))
        inv = inv[0] if len(inv) == 1 else jnp.concatenate(inv, axis=1)
        return _dot_nt(qa_ref[...], cc_ref[:n_keys, :]) * inv

    a = q_ref[...].astype(F32).reshape(rows, HEAD_PAD)
    an = jnp.where(lane < nope, a, 0.0)
    head = row // t_new
    qbd = jnp.concatenate([jnp.where(head == h, an, 0.0) for h in range(n_heads)], axis=1).astype(BF)
    qa_ref[...] = _dot(qbd, wabs_ref[...]).astype(BF)
    qr_ref[...] = jnp.where(lane < rope, pltpu.roll(a, HEAD_PAD - nope, 1), 0.0).astype(BF)
    m_ref[...] = jnp.full_like(m_ref, NEG)
    l_ref[...] = jnp.zeros_like(l_ref)
    acc_ref[...] = jnp.zeros_like(acc_ref)
    cc_ref[:page, :] = jnp.concatenate(
        [c2_ref[...], jnp.zeros((page - t_new, c2_ref.shape[1]), F32)], axis=0).astype(BF)
    rn_ref[...] = jnp.zeros_like(rn_ref)
    rn_ref[:t_new, :rope] = r2_ref[...].astype(BF)
    keep = (lane < t_new) & (lane <= row % t_new)
    s = jnp.where(keep, nope_scores(page) + _dot_nt(qr_ref[...], rn_ref[...]), NEG)
    _softmax_step(s, cc_ref[:page, :], m_ref, l_ref, acc_ref)

    def compute(c, slot):
        for p in range(pps):
            cc_ref[page * p:page * (p + 1), :] = cbuf[slot, p].astype(BF)
            rc_ref[:, page * p:page * (p + 1)] = rbuf[slot, p].astype(BF)
        s = nope_scores(pps * page) + _dot(qr_ref[:, :rope], rc_ref[...])
        _softmax_step(s, cc_ref[...], m_ref, l_ref, acc_ref)

    _chunk_pipeline(pt_ref, (cc_hbm, cr_hbm), (cbuf, rbuf), sems, layer=layer, n_pages=n_pages,
                    pps=pps, nch=nch, reverse=False, compute=compute)

    olat = (acc_ref[...] / l_ref[...]).astype(BF)
    of = _dot(olat, wuv_ref[...])
    row2 = lax.broadcasted_iota(jnp.int32, of.shape, 0)
    lane2 = lax.broadcasted_iota(jnp.int32, of.shape, 1)
    om = jnp.where(lane2 // vdim == row2 // t_new, of, 0.0).astype(BF)
    o_ref[...] = _dot(sel_ref[...], om)


def _mla_decode(pt, q2, ckv2, kpe2, cache_c, cache_r, layer, w, *, t_new, pps):
    n_heads, ns, _ = q2.shape
    b2 = ns // t_new
    _, n_pool, page, kv_lora = cache_c.shape
    rope, nope, vdim = w["rope"], w["nope"], w["vdim"]
    n_pages = pt.shape[0] // b2
    nch = n_pages // pps
    assert nch % 2 == 0
    rows = n_heads * t_new
    rq = np.zeros((rows, LANES), np.float32)
    sel = np.zeros((t_new, rows), np.float32)
    for r in range(rows):
        rq[r, np.arange(LANES) % n_heads == r // t_new] = 1.0
        sel[r % t_new, r] = 1.0
    consts = [w["wuk_dh"], jnp.asarray(rq, BF), w["wabs"], w["wuv_flat"], jnp.asarray(sel, BF)]
    kern = functools.partial(_mla_decode_kernel, layer=layer, n_pages=n_pages, pps=pps, nch=nch, t_new=t_new,
                             n_heads=n_heads, nope=nope, rope=rope, vdim=vdim, kblk=min(4, pps) * page)
    grid_spec = pltpu.PrefetchScalarGridSpec(
        num_scalar_prefetch=1,
        grid=(b2,),
        in_specs=[
            pl.BlockSpec((n_heads, t_new, HEAD_PAD), lambda b, ptr: (0, b, 0)),
            pl.BlockSpec((t_new, kv_lora), lambda b, ptr: (b, 0)),
            pl.BlockSpec((t_new, rope), lambda b, ptr: (b, 0)),
        ] + [pl.BlockSpec(x.shape, lambda b, ptr, nd=x.ndim: (0,) * nd) for x in consts]
        + [pl.BlockSpec(memory_space=pl.ANY)] * 2,
        out_specs=pl.BlockSpec((t_new, n_heads * vdim), lambda b, ptr: (b, 0)),
        scratch_shapes=[
            pltpu.VMEM((2, pps, page, kv_lora), F32), pltpu.VMEM((2, pps, rope, page), F32),
            pltpu.SemaphoreType.DMA((2, 2)),
            pltpu.VMEM((rows, kv_lora), BF), pltpu.VMEM((rows, LANES), BF),
            pltpu.VMEM((rows, 1), F32), pltpu.VMEM((rows, 1), F32), pltpu.VMEM((rows, kv_lora), F32),
            pltpu.VMEM((pps * page, kv_lora), BF), pltpu.VMEM((rope, pps * page), BF),
            pltpu.VMEM((page, LANES), BF),
        ],
    )
    return pl.pallas_call(
        kern, grid_spec=grid_spec,
        out_shape=jax.ShapeDtypeStruct((ns, n_heads * vdim), F32),
        compiler_params=_params(("arbitrary",)),
        name="mla_decode",
    )(pt, q2, ckv2, kpe2, *consts, cache_c, cache_r)


TM = 512
TQ = 256
TKC = 512
PPS = 32


def _ffn_chunk(dff):
    for parts in (2, 1, 4, 11, 22):
        if dff % parts == 0 and (dff // parts) % LANES == 0:
            return dff // parts
    return dff


def kernel(x_prompt, x_sample, cache_fox_k, cache_fox_v, cache_fox_logf, cache_mla_ckv, cache_mla_kpe, page_table, attn_norm, ffn_norm, fox_w_in, fox_b_f, fox_g_q, fox_g_k, fox_w_o, mla_w_in, mla_g_cq, mla_w_uq, mla_g_qn, mla_g_qr, mla_g_ckv, mla_g_kr, mla_w_ukv, mla_g_kn, mla_w_o, ffn_w_gu, ffn_w_down):
    b1, seq, d = x_prompt.shape
    b2, t_new, _ = x_sample.shape
    depth = attn_norm.shape[0]
    n_fox, n_pool, page, fox_kv, fox_hd = cache_fox_k.shape
    fox_heads = fox_b_f.shape[1]
    past = page_table.shape[1] * page
    dff = ffn_w_down.shape[1]
    tf = _ffn_chunk(dff)
    n_p, n_s = b1 * seq, b2 * t_new
    tm_s = min(TM, n_s)
    assert fox_hd == 64 and seq % TKC == 0 and seq % TM == 0 and n_s % tm_s == 0 and tm_s % t_new == 0
    assert page_table.shape[1] % PPS == 0 and page == LANES and t_new == SUBLANES

    hp = x_prompt.reshape(n_p, d)
    hs = x_sample.reshape(n_s, d)
    pt = page_table.reshape(-1).astype(jnp.int32)
    ck4 = jnp.transpose(cache_fox_k, (0, 1, 3, 4, 2)).reshape(n_fox, n_pool, fox_kv * fox_hd, page)
    cv4 = jnp.transpose(cache_fox_v, (0, 1, 3, 4, 2)).reshape(n_fox, n_pool, fox_kv * fox_hd, page)
    clf = jnp.transpose(cache_fox_logf, (0, 1, 3, 2))
    ckr = jnp.transpose(cache_mla_kpe, (0, 1, 3, 2))

    tri_p = jnp.asarray(np.tril(np.ones((TM, TM), np.float32)), BF)
    seq_id = np.arange(tm_s) // t_new
    tri_s = jnp.asarray(np.tril(np.ones((tm_s, tm_s), np.float32)) * (seq_id[:, None] == seq_id[None, :]), BF)
    rope_half = mla_g_kr.shape[1] // 2
    cos_p, sin_p = _rope_tables(jnp.arange(seq), rope_half)
    cos_s, sin_s = _rope_tables(past + jnp.arange(tm_s) % t_new, rope_half)

    fk_p, fv_p, fl_p, fk_s, fv_s, fl_s = [], [], [], [], [], []
    mc_p, mr_p, mc_s, mr_s = [], [], [], []
    for i in range(depth):
        j = i // 2
        if i % 2 == 0:
            w = _fox_weights(fox_w_in[j], fox_b_f[j], fox_g_q[j], fox_g_k[j], attn_norm[i], fox_heads, fox_kv, fox_hd)
            q, ka, vp, k, v, lf, _ = _fox_in(hp, w, tm=TM, tiles_per_seq=seq // TM, tri=tri_p)
            o_p = _flash_prompt(q, ka, vp, batch=b1, n_q=fox_heads // fox_kv, n_k=1, tq=TQ, tkc=TKC, vdim=fox_hd)
            q2, _, _, k2, v2, lf2, c2 = _fox_in(hs, w, tm=tm_s, tiles_per_seq=1, tri=tri_s)
            o_s = _fox_decode(pt, q2, c2, k2, v2, ck4, cv4, clf, j, t_new=t_new, pps=PPS)
            fk_p.append(k); fv_p.append(v); fl_p.append(lf)
            fk_s.append(k2); fv_s.append(v2); fl_s.append(lf2)
            wo = fox_w_o[j].astype(BF)
        else:
            w = _mla_weights(mla_w_in[j], mla_g_cq[j], mla_w_uq[j], mla_g_qn[j], mla_g_qr[j], mla_g_ckv[j],
                             mla_g_kr[j], mla_w_ukv[j], mla_g_kn[j], attn_norm[i])
            q, kk, vv, ckv, kpe = _mla_in(hp, cos_p, sin_p, w, tm=TM)
            o_p = _flash_prompt(q, kk, vv, batch=b1, n_q=2, n_k=2, tq=TKC, tkc=TKC, vdim=w["vdim"])
            q2, _, _, ckv2, kpe2 = _mla_in(hs, cos_s, sin_s, w, tm=tm_s)
            o_s = _mla_decode(pt, q2, ckv2, kpe2, cache_mla_ckv, ckr, j, w, t_new=t_new, pps=PPS)
            mc_p.append(ckv); mr_p.append(kpe)
            mc_s.append(ckv2); mr_s.append(kpe2)
            wo = mla_w_o[j].astype(BF)
        gf = ffn_norm[i].reshape(1, d)
        wg = ffn_w_gu[i, :, :dff].astype(BF)
        wu = ffn_w_gu[i, :, dff:].astype(BF)
        wd = ffn_w_down[i].astype(BF)
        hp = _out_ffn(hp, o_p, wo, gf, wg, wu, wd, tm=TM, tf=tf)
        hs = _out_ffn(hs, o_s, wo, gf, wg, wu, wd, tm=tm_s, tf=tf)

    def stk(xs, lead, tail):
        return jnp.stack(xs).reshape((len(xs),) + lead + tail)

    return (hp.reshape(b1, seq, d), hs.reshape(b2, t_new, d),
            stk(fk_p, (b1, seq), (fox_kv, fox_hd)), stk(fv_p, (b1, seq), (fox_kv, fox_hd)),
            stk(fl_p, (b1, seq), (fox_heads,)),
            stk(mc_p, (b1, seq), (mla_g_ckv.shape[1],)), stk(mr_p, (b1, seq), (mla_g_kr.shape[1],)),
            stk(fk_s, (b2, t_new), (fox_kv, fox_hd)), stk(fv_s, (b2, t_new), (fox_kv, fox_hd)),
            stk(fl_s, (b2, t_new), (fox_heads,)),
            stk(mc_s, (b2, t_new), (mla_g_ckv.shape[1],)), stk(mr_s, (b2, t_new), (mla_g_kr.shape[1],)))
```

```python
import functools

import numpy as np
import jax
import jax.numpy as jnp
from jax import lax
from jax.experimental import pallas as pl
from jax.experimental.pallas import tpu as pltpu

BF = jnp.bfloat16
F32 = jnp.float32
EPS = 1e-6
ROPE_THETA = 10000.0
LANES = 128
SUBLANES = 8
HEAD_PAD = LANES
NEG = -1e30
LOG2E = 1.4426950408889634
VMEM_LIMIT = 56 * 1024 * 1024
NT = (((1,), (1,)), ((), ()))


def _dot(a, b):
    return jnp.dot(a, b, preferred_element_type=F32)


def _dot_nt(a, b):
    return lax.dot_general(a, b, NT, preferred_element_type=F32)


def _split3(x):
    hi = x.astype(BF)
    r1 = x - hi.astype(F32)
    mid = r1.astype(BF)
    lo = (r1 - mid.astype(F32)).astype(BF)
    return hi, mid, lo


def _dot_exact_lhs(x, w):
    hi, mid, lo = _split3(x)
    return _dot(hi, w) + _dot(mid, w) + _dot(lo, w)


def _dot_exact_rhs(w, x):
    hi, mid, lo = _split3(x)
    return _dot(w, hi) + _dot(w, mid) + _dot(w, lo)


def _rms_rows(x, g):
    return x * lax.rsqrt(jnp.mean(x * x, axis=-1, keepdims=True) + EPS) * g


def _log_sigmoid(x):
    return jnp.minimum(x, 0.0) - jnp.log1p(jnp.exp(-jnp.abs(x)))


def _full(shape):
    n = len(shape)
    return pl.BlockSpec(shape, lambda *_: (0,) * n)


def _params(sem):
    return pltpu.CompilerParams(dimension_semantics=sem, vmem_limit_bytes=VMEM_LIMIT)


def _fox_in_kernel(x_ref, ga_ref, wq_ref, wk_ref, wv_ref, wkv_ref, wf_ref, bf_ref, gq_ref, gk_ref, gk2_ref,
                   g1_ref, g2_ref, tri_ref, pq_ref, pk_ref, oq_ref, ok_ref,
                   q_out, ka_out, vp_out, k_out, v_out, lf_out, c_out, carry_ref,
                   *, tiles_per_seq, n_heads, n_kv, n_gate, hd):
    i = pl.program_id(0)
    tm = x_ref.shape[0]
    xn = _rms_rows(x_ref[...], ga_ref[...]).astype(BF)
    lane = lax.broadcasted_iota(jnp.int32, (tm, LANES), 1)

    lf = jnp.where(lane < n_gate, _log_sigmoid(_dot(xn, wf_ref[...]) + bf_ref[...]), 0.0)
    lf_out[...] = lf[:, :n_gate]

    @pl.when(i % tiles_per_seq == 0)
    def _():
        carry_ref[...] = jnp.zeros_like(carry_ref)

    c = _dot_exact_rhs(tri_ref[...], lf) + carry_ref[...]
    carry_ref[...] = c[tm - 1:tm, :]
    c = c * LOG2E
    c_out[...] = c

    ch, cm, cl = _split3(c)
    cparts = (ch.astype(F32) + pltpu.roll(cm.astype(F32), 16, 1) + pltpu.roll(cl.astype(F32), 32, 1)).astype(BF)

    g1 = g1_ref[...]
    v_ones = jnp.where(lane >= hd, 1.0, 0.0)
    for j in range(n_heads // 2):
        sl = slice(2 * HEAD_PAD * j, 2 * HEAD_PAD * (j + 1))
        zq = _dot(xn, wq_ref[:, sl])
        aug = _dot(cparts, pq_ref[:, sl]) + oq_ref[:, sl]
        for u in range(2):
            z = zq[:, HEAD_PAD * u:HEAD_PAD * (u + 1)]
            ms = _dot((z * z).astype(BF), g1)
            qn = z * lax.rsqrt(ms + EPS) * gq_ref[...]
            q_out[2 * j + u] = (qn + aug[:, HEAD_PAD * u:HEAD_PAD * (u + 1)]).astype(BF)

    for j in range(n_kv // 2):
        sl = slice(2 * HEAD_PAD * j, 2 * HEAD_PAD * (j + 1))
        zk = _dot(xn, wk_ref[:, sl])
        zv = _dot(xn, wv_ref[:, sl])
        aug = _dot(cparts, pk_ref[:, sl]) + ok_ref[:, sl]
        for u in range(2):
            z = zk[:, HEAD_PAD * u:HEAD_PAD * (u + 1)]
            ms = _dot((z * z).astype(BF), g1)
            kn = z * lax.rsqrt(ms + EPS) * gk_ref[...]
            ka_out[2 * j + u] = (kn + aug[:, HEAD_PAD * u:HEAD_PAD * (u + 1)]).astype(BF)
            vp_out[2 * j + u] = (zv[:, HEAD_PAD * u:HEAD_PAD * (u + 1)] + v_ones).astype(BF)

    zkv = _dot(xn, wkv_ref[...])
    nk = wkv_ref.shape[1] // 2
    k = zkv[:, :nk]
    ms = _dot((k * k).astype(BF), g2_ref[...])
    k_out[...] = k * lax.rsqrt(ms + EPS) * gk2_ref[...]
    v_out[...] = zkv[:, nk:]


def _fox_in(x, w, *, tm, tiles_per_seq, tri):
    n, d = x.shape
    n_heads, n_kv, n_gate, hd = w["n_heads"], w["n_kv"], w["n_gate"], w["head_dim"]
    kern = functools.partial(_fox_in_kernel, tiles_per_seq=tiles_per_seq, n_heads=n_heads, n_kv=n_kv, n_gate=n_gate,
                             hd=hd)
    consts = [w["ga"], w["wq"], w["wk"], w["wv"], w["wkv"], w["wf"], w["bf"], w["gq"], w["gk"], w["gk2"],
              w["g1"], w["g2"], tri, w["pq"], w["pk"], w["oq"], w["ok"]]
    return pl.pallas_call(
        kern,
        grid=(n // tm,),
        in_specs=[pl.BlockSpec((tm, d), lambda i: (i, 0))] + [_full(c.shape) for c in consts],
        out_specs=[
            pl.BlockSpec((n_heads, tm, HEAD_PAD), lambda i: (0, i, 0)),
            pl.BlockSpec((n_kv, tm, HEAD_PAD), lambda i: (0, i, 0)),
            pl.BlockSpec((n_kv, tm, HEAD_PAD), lambda i: (0, i, 0)),
            pl.BlockSpec((tm, n_kv * hd), lambda i: (i, 0)),
            pl.BlockSpec((tm, n_kv * hd), lambda i: (i, 0)),
            pl.BlockSpec((tm, n_gate), lambda i: (i, 0)),
            pl.BlockSpec((tm, LANES), lambda i: (i, 0)),
        ],
        out_shape=[
            jax.ShapeDtypeStruct((n_heads, n, HEAD_PAD), BF),
            jax.ShapeDtypeStruct((n_kv, n, HEAD_PAD), BF),
            jax.ShapeDtypeStruct((n_kv, n, HEAD_PAD), BF),
            jax.ShapeDtypeStruct((n, n_kv * hd), F32),
            jax.ShapeDtypeStruct((n, n_kv * hd), F32),
            jax.ShapeDtypeStruct((n, n_gate), F32),
            jax.ShapeDtypeStruct((n, LANES), F32),
        ],
        scratch_shapes=[pltpu.VMEM((1, LANES), F32)],
        compiler_params=_params(("arbitrary",)),
        name="fox_in",
    )(x, *consts)


def _fox_weights(w_in, b_f, g_q, g_k, g_attn, n_heads, n_kv, hd):
    d = w_in.shape[0]
    nq, nk = n_heads * hd, n_kv * hd
    n_gate = n_heads
    grp = n_heads // n_kv
    scale = hd ** -0.5
    pad = HEAD_PAD - hd

    def heads_padded(wm, nh):
        return jnp.pad(wm.reshape(d, nh, hd), ((0, 0), (0, 0), (0, pad))).reshape(d, nh * HEAD_PAD).astype(BF)

    ck0, cq0 = hd, hd + 3 * grp
    pq = np.zeros((LANES, n_heads * HEAD_PAD), np.float32)
    oq = np.zeros((1, n_heads * HEAD_PAD), np.float32)
    pk = np.zeros((LANES, n_kv * HEAD_PAD), np.float32)
    ok = np.zeros((1, n_kv * HEAD_PAD), np.float32)
    for h in range(n_heads):
        kv, g = divmod(h, grp)
        for p in range(3):
            pq[16 * p + h, HEAD_PAD * h + cq0 + p] = 1.0
            oq[0, HEAD_PAD * h + ck0 + 3 * g + p] = 1.0
            pk[16 * p + h, HEAD_PAD * kv + ck0 + 3 * g + p] = -1.0
    for kv in range(n_kv):
        for p in range(3):
            ok[0, HEAD_PAD * kv + cq0 + p] = 1.0
    g1 = np.zeros((HEAD_PAD, HEAD_PAD), np.float32)
    g1[:hd, :hd] = 1.0 / hd
    g2 = np.kron(np.eye(n_kv, dtype=np.float32), np.full((hd, hd), 1.0 / hd, np.float32))
    return dict(
        n_heads=n_heads, n_kv=n_kv, n_gate=n_gate, head_dim=hd,
        ga=g_attn.reshape(1, d),
        wq=heads_padded(w_in[:, :nq], n_heads),
        wk=heads_padded(w_in[:, nq:nq + nk], n_kv),
        wv=heads_padded(w_in[:, nq + nk:nq + 2 * nk], n_kv),
        wkv=w_in[:, nq:nq + 2 * nk].astype(BF),
        wf=jnp.pad(w_in[:, nq + 2 * nk:], ((0, 0), (0, LANES - n_gate))).astype(BF),
        bf=jnp.pad(b_f, (0, LANES - n_gate)).reshape(1, LANES),
        gq=jnp.pad(g_q * (scale * LOG2E), (0, pad)).reshape(1, HEAD_PAD),
        gk=jnp.pad(g_k, (0, pad)).reshape(1, HEAD_PAD),
        gk2=jnp.tile(g_k, n_kv).reshape(1, nk),
        g1=jnp.asarray(g1, BF), g2=jnp.asarray(g2, BF),
        pq=jnp.asarray(pq, BF), pk=jnp.asarray(pk, BF), oq=jnp.asarray(oq), ok=jnp.asarray(ok),
    )


def _rope_lanes(x, cos, sin, lane, lo, half):
    nl = x.shape[-1]
    fwd = pltpu.roll(x, nl - half, 1)
    bwd = pltpu.roll(x, half, 1)
    first = (lane >= lo) & (lane < lo + half)
    second = (lane >= lo + half) & (lane < lo + 2 * half)
    rot = jnp.where(first, -fwd, jnp.where(second, bwd, 0.0))
    return jnp.where(first | second, x * cos + rot * sin, x)


def _mla_in_kernel(x_ref, cos_ref, sin_ref, ga_ref, win_ref, gcq_ref, gckv_ref, gkr_ref, wuq_ref, gq_ref,
                   wuk_ref, gk_ref, wuv_ref, g1_ref,
                   q_out, kk_out, vv_out, ckv_out, kpe_out,
                   *, n_heads, q_lora, kv_lora, nope, rope, vdim):
    tm = x_ref.shape[0]
    lane = lax.broadcasted_iota(jnp.int32, (tm, LANES), 1)
    cos, sin = cos_ref[...], sin_ref[...]
    xn = _rms_rows(x_ref[...], ga_ref[...]).astype(BF)
    z = _dot(xn, win_ref[...])
    cq = _rms_rows(z[:, :q_lora], gcq_ref[...]).astype(BF)
    ckv = _rms_rows(z[:, q_lora:q_lora + kv_lora], gckv_ref[...])
    ckv_out[...] = ckv
    ckv_b = ckv.astype(BF)
    kr = z[:, q_lora + kv_lora:]
    kr = kr * lax.rsqrt(jnp.sum(kr * kr, axis=-1, keepdims=True) * (1.0 / rope) + EPS) * gkr_ref[...]
    kpe = _rope_lanes(kr, cos, sin, lane, 0, rope // 2)
    kpe_out[...] = kpe[:, :rope]
    kpe_sh = pltpu.roll(kpe, nope, 1)

    g1 = g1_ref[...]
    v_ones = jnp.where(lane >= vdim, 1.0, 0.0)
    for j in range(n_heads // 2):
        sl = slice(2 * HEAD_PAD * j, 2 * HEAD_PAD * (j + 1))
        zq = _dot(cq, wuq_ref[:, sl])
        zk = _dot(ckv_b, wuk_ref[:, sl])
        zv = _dot(ckv_b, wuv_ref[:, sl])
        for u in range(2):
            bl = slice(HEAD_PAD * u, HEAD_PAD * (u + 1))
            q = zq[:, bl]
            q = q * lax.rsqrt(_dot((q * q).astype(BF), g1) + EPS) * gq_ref[...]
            q_out[2 * j + u] = _rope_lanes(q, cos, sin, lane, nope, rope // 2).astype(BF)
            k = zk[:, bl]
            k = k * lax.rsqrt(_dot((k * k).astype(BF), g1) + EPS) * gk_ref[...]
            kk_out[2 * j + u] = (k + kpe_sh).astype(BF)
            vv_out[2 * j + u] = (zv[:, bl] + v_ones).astype(BF)


def _mla_in(x, cos, sin, w, *, tm):
    n, d = x.shape
    n_heads, kv_lora, rope = w["n_heads"], w["kv_lora"], w["rope"]
    tbl_tiles = cos.shape[0] // tm
    kern = functools.partial(_mla_in_kernel, n_heads=n_heads, q_lora=w["q_lora"], kv_lora=kv_lora,
                             nope=w["nope"], rope=rope, vdim=w["vdim"])
    consts = [w["ga"], w["win"], w["gcq"], w["gckv"], w["gkr"], w["wuq"], w["gq"], w["wuk"], w["gk"], w["wuv"], w["g1"]]
    head_spec = pl.BlockSpec((n_heads, tm, HEAD_PAD), lambda i: (0, i, 0))
    head_shape = jax.ShapeDtypeStruct((n_heads, n, HEAD_PAD), BF)
    return pl.pallas_call(
        kern,
        grid=(n // tm,),
        in_specs=[pl.BlockSpec((tm, d), lambda i: (i, 0)),
                  pl.BlockSpec((tm, LANES), lambda i: (i % tbl_tiles, 0)),
                  pl.BlockSpec((tm, LANES), lambda i: (i % tbl_tiles, 0))] + [_full(c.shape) for c in consts],
        out_specs=[head_spec, head_spec, head_spec,
                   pl.BlockSpec((tm, kv_lora), lambda i: (i, 0)),
                   pl.BlockSpec((tm, rope), lambda i: (i, 0))],
        out_shape=[head_shape, head_shape, head_shape,
                   jax.ShapeDtypeStruct((n, kv_lora), F32),
                   jax.ShapeDtypeStruct((n, rope), F32)],
        compiler_params=_params(("parallel",)),
        name="mla_in",
    )(x, cos, sin, *consts)


def _mla_weights(w_in, g_cq, w_uq, g_qn, g_qr, g_ckv, g_kr, w_ukv, g_kn, g_attn):
    d = w_in.shape[0]
    q_lora, kv_lora, rope, nope = g_cq.shape[0], g_ckv.shape[0], g_kr.shape[0], g_qn.shape[0]
    n_heads = w_uq.shape[1]
    vdim = w_ukv.shape[2] - nope
    scale = (nope + rope) ** -0.5
    in_pad = -w_in.shape[1] % LANES
    w_uk, w_uv = w_ukv[..., :nope], w_ukv[..., nope:]

    def heads_padded(wm):
        r, nh, hd = wm.shape
        return jnp.pad(wm, ((0, 0), (0, 0), (0, HEAD_PAD - hd))).reshape(r, nh * HEAD_PAD).astype(BF)

    g1 = np.zeros((HEAD_PAD, HEAD_PAD), np.float32)
    g1[:nope, :nope] = 1.0 / nope
    g1[nope:nope + rope, nope:nope + rope] = 1.0 / rope
    wuk_dh = jnp.transpose(w_uk, (0, 2, 1)).reshape(kv_lora, nope * n_heads).astype(BF)
    wabs = jnp.transpose(w_uk * g_kn[None, None, :], (1, 2, 0))
    wabs = jnp.pad(wabs, ((0, 0), (0, HEAD_PAD - nope), (0, 0))).reshape(n_heads * HEAD_PAD, kv_lora).astype(BF)
    return dict(
        n_heads=n_heads, q_lora=q_lora, kv_lora=kv_lora, rope=rope, nope=nope, vdim=vdim,
        ga=g_attn.reshape(1, d),
        win=jnp.pad(w_in, ((0, 0), (0, in_pad))).astype(BF),
        gcq=g_cq.reshape(1, -1), gckv=g_ckv.reshape(1, -1),
        gkr=jnp.pad(g_kr, (0, LANES - rope)).reshape(1, LANES),
        wuq=heads_padded(w_uq),
        gq=jnp.pad(jnp.concatenate([g_qn, g_qr]) * (scale * LOG2E), (0, HEAD_PAD - nope - rope)).reshape(1, HEAD_PAD),
        wuk=heads_padded(w_uk), gk=jnp.pad(g_kn, (0, HEAD_PAD - nope)).reshape(1, HEAD_PAD),
        wuv=heads_padded(w_uv), g1=jnp.asarray(g1, BF),
        wuk_dh=wuk_dh, wabs=wabs, wuv_flat=w_uv.reshape(kv_lora, n_heads * vdim).astype(BF),
    )


def _rope_tables(pos, half):
    inv = ROPE_THETA ** (-jnp.arange(half, dtype=F32) / half)
    ang = pos.astype(F32)[:, None] * inv[None, :]
    reps = LANES // half
    return jnp.tile(jnp.cos(ang), (1, reps)), jnp.tile(jnp.sin(ang), (1, reps))


def _flash_kernel(q_ref, k_ref, v_ref, o_ref, s_ref, m_ref, acc_ref, *, n_q, n_k, vdim, tkc):
    qi = pl.program_id(2)
    tq = q_ref.shape[1]
    diag = (qi * tq) // tkc
    off = qi * tq - diag * tkc
    lane_tiles = tkc // LANES

    m_ref[...] = jnp.full_like(m_ref, NEG)
    acc_ref[...] = jnp.zeros_like(acc_ref)

    def lane_max(s):
        m = s[:, :LANES]
        for t in range(1, lane_tiles):
            m = jnp.maximum(m, s[:, LANES * t:LANES * (t + 1)])
        return m

    def score_chunk(kc, keep):
        k0 = pl.multiple_of(kc * tkc, tkc)
        for r in range(n_q):
            s = _dot_nt(q_ref[r], k_ref[r * n_k // n_q, pl.ds(k0, tkc), :])
            if keep is not None:
                s = jnp.where(keep, s, NEG)
            s_ref[r, kc] = s
            m_ref[r] = jnp.maximum(m_ref[r], lane_max(s))

    def pass_a(kc, carry):
        score_chunk(kc, None)
        return carry

    lax.fori_loop(0, diag, pass_a, 0)
    row = lax.broadcasted_iota(jnp.int32, (tq, tkc), 0)
    col = lax.broadcasted_iota(jnp.int32, (tq, tkc), 1)
    score_chunk(diag, col <= row + off)

    for r in range(n_q):
        m_ref[r] = jnp.broadcast_to(jnp.max(m_ref[r], axis=-1, keepdims=True), (tq, LANES))

    def pass_b(kc, carry):
        k0 = pl.multiple_of(kc * tkc, tkc)
        for r in range(n_q):
            m = m_ref[r]
            p = jnp.exp2(s_ref[r, kc] - jnp.concatenate([m] * lane_tiles, axis=1)).astype(BF)
            acc_ref[r] += _dot(p, v_ref[r * n_k // n_q, pl.ds(k0, tkc), :])
        return carry

    lax.fori_loop(0, diag + 1, pass_b, 0)

    lane = lax.broadcasted_iota(jnp.int32, (tq, LANES), 1)

    def normalised(r):
        a = acc_ref[r]
        return jnp.where(lane < vdim, a / jnp.where(lane < vdim, pltpu.roll(a, vdim, 1), 1.0), 0.0)

    for j in range(n_q // 2):
        o_ref[:, LANES * j:LANES * (j + 1)] = (
            normalised(2 * j) + pltpu.roll(normalised(2 * j + 1), vdim, 1)).astype(o_ref.dtype)


def _flash_prompt(q, k, v, *, batch, n_q, n_k, tq, tkc, vdim):
    hq, n, _ = q.shape
    seq = n // batch
    nb = seq // tq
    assert tkc % tq == 0 and seq % tkc == 0 and 2 * vdim == LANES
    kern = functools.partial(_flash_kernel, n_q=n_q, n_k=n_k, vdim=vdim, tkc=tkc)
    return pl.pallas_call(
        kern,
        grid=(batch, hq // n_q, nb),
        in_specs=[
            pl.BlockSpec((n_q, tq, HEAD_PAD), lambda b, j, i: (j, b * nb + i, 0)),
            pl.BlockSpec((n_k, seq, HEAD_PAD), lambda b, j, i: (j, b, 0)),
            pl.BlockSpec((n_k, seq, HEAD_PAD), lambda b, j, i: (j, b, 0)),
        ],
        out_specs=pl.BlockSpec((tq, n_q * vdim), lambda b, j, i: (b * nb + i, j)),
        out_shape=jax.ShapeDtypeStruct((n, hq * vdim), BF),
        scratch_shapes=[pltpu.VMEM((n_q, seq // tkc, tq, tkc), F32), pltpu.VMEM((n_q, tq, LANES), F32),
                        pltpu.VMEM((n_q, tq, HEAD_PAD), F32)],
        compiler_params=_params(("parallel", "parallel", "arbitrary")),
        name="flash_prompt",
    )(q, k, v)


def _out_ffn_kernel(h_ref, o_ref, wo_ref, gf_ref, wg_ref, wu_ref, wd_ref, out_ref, h1_ref, xn_ref, acc_ref):
    f = pl.program_id(1)

    @pl.when(f == 0)
    def _():
        h1 = h_ref[...] + _dot(o_ref[...].astype(BF), wo_ref[...])
        h1_ref[...] = h1
        xn_ref[...] = _rms_rows(h1, gf_ref[...]).astype(BF)
        acc_ref[...] = jnp.zeros_like(acc_ref)

    xn = xn_ref[...]
    g = _dot(xn, wg_ref[...])
    u = _dot(xn, wu_ref[...])
    acc_ref[...] += _dot((g * jax.nn.sigmoid(g) * u).astype(BF), wd_ref[...])

    @pl.when(f == pl.num_programs(1) - 1)
    def _():
        out_ref[...] = h1_ref[...] + acc_ref[...]


def _out_ffn(h, o, wo, gf, wg, wu, wd, *, tm, tf):
    n, d = h.shape
    do = o.shape[1]
    dff = wg.shape[1]
    return pl.pallas_call(
        _out_ffn_kernel,
        grid=(n // tm, dff // tf),
        in_specs=[
            pl.BlockSpec((tm, d), lambda i, f: (i, 0)),
            pl.BlockSpec((tm, do), lambda i, f: (i, 0)),
            pl.BlockSpec((do, d), lambda i, f: (0, 0)),
            pl.BlockSpec((1, d), lambda i, f: (0, 0)),
            pl.BlockSpec((d, tf), lambda i, f: (0, f)),
            pl.BlockSpec((d, tf), lambda i, f: (0, f)),
            pl.BlockSpec((tf, d), lambda i, f: (f, 0)),
        ],
        out_specs=pl.BlockSpec((tm, d), lambda i, f: (i, 0)),
        out_shape=jax.ShapeDtypeStruct((n, d), F32),
        scratch_shapes=[pltpu.VMEM((tm, d), F32), pltpu.VMEM((tm, d), BF), pltpu.VMEM((tm, d), F32)],
        compiler_params=_params(("parallel", "arbitrary")),
        name="out_ffn",
    )(h, o, wo, gf, wg, wu, wd)


def _softmax_step(s, v, m_ref, l_ref, acc_ref, v_is_transposed=False):
    m_prev = m_ref[...]
    m_new = jnp.maximum(m_prev, jnp.max(s, axis=-1, keepdims=True))
    alpha = jnp.exp2(m_prev - m_new)
    p = jnp.exp2(s - m_new)
    l_ref[...] = alpha * l_ref[...] + jnp.sum(p, axis=-1, keepdims=True)
    p = p.astype(BF)
    acc_ref[...] = alpha * acc_ref[...] + (_dot_nt(p, v) if v_is_transposed else _dot(p, v))
    m_ref[...] = m_new


def _rows_from_heads(x, t_new):
    return jnp.concatenate([jnp.broadcast_to(x[h:h + 1], (t_new, x.shape[1])) for h in range(x.shape[0])], axis=0)


def _chunk_pipeline(pt_ref, srcs, bufs, sems, *, layer, n_pages, pps, nch, reverse, compute):
    b = pl.program_id(0)
    n_seq = pl.num_programs(0)

    def chunk_copies(seq, c, slot):
        first = ((nch - 1 - c) if reverse else c) * pps
        out = []
        for p in range(pps):
            pid = pt_ref[seq * n_pages + first + p]
            for i, (src, buf) in enumerate(zip(srcs, bufs)):
                out.append(pltpu.make_async_copy(src.at[layer, pid], buf.at[slot, p], sems.at[i, slot]))
        return out

    @pl.when(b == 0)
    def _():
        for cp in chunk_copies(b, 0, 0):
            cp.start()

    for c in range(nch):
        slot = c % 2
        if c + 1 < nch:
            for cp in chunk_copies(b, c + 1, 1 - slot):
                cp.start()
        else:
            @pl.when(b + 1 < n_seq)
            def _():
                for cp in chunk_copies(b + 1, 0, 1 - slot):
                    cp.start()
        for cp in chunk_copies(b, c, slot):
            cp.wait()
        compute(c, slot)


def _fox_decode_kernel(pt_ref, q_ref, c_ref, k2_ref, v2_ref, u_ref, pg_ref, sg_ref, ck_hbm, cv_hbm, cl_hbm,
                       o_ref, kbuf, vbuf, lbuf, sems, qbd_ref, m_ref, l_ref, acc_ref, kc_ref, vc_ref,
                       *, layer, n_pages, pps, nch, t_new, n_heads, n_kv, hd):
    rows = n_heads * t_new
    page = kbuf.shape[3]
    kvw = n_kv * hd
    row = lax.broadcasted_iota(jnp.int32, (rows, LANES), 0)
    lane = lax.broadcasted_iota(jnp.int32, (rows, LANES), 1)

    a = q_ref[...].astype(F32).reshape(rows, HEAD_PAD)
    a = jnp.where(lane < hd, a, 0.0)
    kvh = row // (rows // n_kv)
    for blk in range(kvw // LANES):
        parts = 0.0
        for u in range(LANES // hd):
            parts = parts + jnp.where(kvh == blk * (LANES // hd) + u, pltpu.roll(a, hd * u, 1) if u else a, 0.0)
        qbd_ref[:, LANES * blk:LANES * (blk + 1)] = parts.astype(BF)
    cn = c_ref[...]
    cn_rows = jnp.concatenate([cn] * n_heads, axis=0)
    cn_row = jnp.sum(jnp.where(lane == row // t_new, cn_rows, 0.0), axis=-1, keepdims=True)
    m_ref[...] = jnp.full_like(m_ref, NEG)
    l_ref[...] = jnp.zeros_like(l_ref)
    acc_ref[...] = jnp.zeros_like(acc_ref)
    zpad = jnp.zeros((page - t_new, kvw), F32)
    kn = jnp.concatenate([k2_ref[...], zpad], axis=0).astype(BF)
    vn = jnp.concatenate([v2_ref[...], zpad], axis=0).astype(BF)
    cn_pad = jnp.concatenate([cn, jnp.zeros((page - t_new, LANES), F32)], axis=0)
    cn_t = cn_pad.T[:n_heads]
    bias = cn_row - _rows_from_heads(cn_t, t_new)
    keep = (lane < t_new) & (lane <= row % t_new)
    s = jnp.where(keep, _dot_nt(qbd_ref[...], kn) + bias, NEG)
    _softmax_step(s, vn, m_ref, l_ref, acc_ref)

    u_mat = u_ref[...]
    carry = [jnp.zeros((n_heads, 1), F32)]

    def compute(c, slot):
        lf_t = jnp.concatenate([lbuf[slot, p] for p in range(pps)], axis=0)
        within = _dot_exact_lhs(lf_t, u_mat)
        total = jnp.sum(lf_t, axis=-1, keepdims=True)
        sfx = [None] * pps
        for p in reversed(range(pps)):
            sfx[p] = within[n_heads * p:n_heads * (p + 1)] + carry[0]
            carry[0] = carry[0] + total[n_heads * p:n_heads * (p + 1)]
            kc_ref[:, page * p:page * (p + 1)] = kbuf[slot, p].astype(BF)
            vc_ref[:, page * p:page * (p + 1)] = vbuf[slot, p].astype(BF)
        bias = _rows_from_heads(jnp.concatenate(sfx, axis=1) * LOG2E, t_new) + cn_row
        s = _dot(qbd_ref[...], kc_ref[...]) + bias
        _softmax_step(s, vc_ref[...], m_ref, l_ref, acc_ref, v_is_transposed=True)

    _chunk_pipeline(pt_ref, (ck_hbm, cv_hbm, cl_hbm), (kbuf, vbuf, lbuf), sems, layer=layer, n_pages=n_pages,
                    pps=pps, nch=nch, reverse=True, compute=compute)

    o = acc_ref[...] / l_ref[...]
    row2 = lax.broadcasted_iota(jnp.int32, (rows, kvw), 0)
    lane2 = lax.broadcasted_iota(jnp.int32, (rows, kvw), 1)
    om = jnp.where(lane2 // hd == row2 // (rows // n_kv), o, 0.0).astype(BF)
    out = jnp.zeros(o_ref.shape, F32)
    for g in range(n_heads // n_kv):
        out = out + _dot(sg_ref[g], _dot(om, pg_ref[g]).astype(BF))
    o_ref[...] = out


def _fox_decode(pt, q2, c2, k2, v2, cache_k, cache_v, cache_lf, layer, *, t_new, pps):
    n_heads, ns, _ = q2.shape
    b2 = ns // t_new
    _, n_pool, kvw, page = cache_k.shape
    n_gate = cache_lf.shape[2]
    n_pages = pt.shape[0] // b2
    nch = n_pages // pps
    assert nch % 2 == 0
    hd = 64
    n_kv = kvw // hd
    grp = n_heads // n_kv
    rows = n_heads * t_new
    u_mat = jnp.asarray(np.tril(np.ones((page, page), np.float32), -1), BF)
    pg = np.zeros((grp, kvw, n_heads * hd), np.float32)
    sg = np.zeros((grp, t_new, rows), np.float32)
    for h in range(n_heads):
        kv, g = divmod(h, grp)
        for dd in range(hd):
            pg[g, kv * hd + dd, h * hd + dd] = 1.0
        for t in range(t_new):
            sg[g, t, h * t_new + t] = 1.0
    consts = [u_mat, jnp.asarray(pg, BF), jnp.asarray(sg, BF)]
    kern = functools.partial(_fox_decode_kernel, layer=layer, n_pages=n_pages, pps=pps, nch=nch, t_new=t_new,
                             n_heads=n_heads, n_kv=n_kv, hd=hd)
    grid_spec = pltpu.PrefetchScalarGridSpec(
        num_scalar_prefetch=1,
        grid=(b2,),
        in_specs=[
            pl.BlockSpec((n_heads, t_new, HEAD_PAD), lambda b, ptr: (0, b, 0)),
            pl.BlockSpec((t_new, LANES), lambda b, ptr: (b, 0)),
            pl.BlockSpec((t_new, kvw), lambda b, ptr: (b, 0)),
            pl.BlockSpec((t_new, kvw), lambda b, ptr: (b, 0)),
        ] + [pl.BlockSpec(x.shape, lambda b, ptr, nd=x.ndim: (0,) * nd) for x in consts]
        + [pl.BlockSpec(memory_space=pl.ANY)] * 3,
        out_specs=pl.BlockSpec((t_new, n_heads * hd), lambda b, ptr: (b, 0)),
        scratch_shapes=[
            pltpu.VMEM((2, pps, kvw, page), F32), pltpu.VMEM((2, pps, kvw, page), F32),
            pltpu.VMEM((2, pps, n_gate, page), F32), pltpu.SemaphoreType.DMA((3, 2)),
            pltpu.VMEM((rows, kvw), BF),
            pltpu.VMEM((rows, 1), F32), pltpu.VMEM((rows, 1), F32), pltpu.VMEM((rows, kvw), F32),
            pltpu.VMEM((kvw, pps * page), BF), pltpu.VMEM((kvw, pps * page), BF),
        ],
    )
    return pl.pallas_call(
        kern, grid_spec=grid_spec,
        out_shape=jax.ShapeDtypeStruct((ns, n_heads * hd), F32),
        compiler_params=_params(("arbitrary",)),
        name="fox_decode",
    )(pt, q2, c2, k2, v2, *consts, cache_k, cache_v, cache_lf)


def _mla_decode_kernel(pt_ref, q_ref, c2_ref, r2_ref, wdh_ref, rq_ref, wabs_ref, wuv_ref, sel_ref, cc_hbm, cr_hbm,
                       o_ref, cbuf, rbuf, sems, qa_ref, qr_ref, m_ref, l_ref, acc_ref, cc_ref, rc_ref, rn_ref,
                       *, layer, n_pages, pps, nch, t_new, n_heads, nope, rope, vdim, kblk):
    rows = n_heads * t_new
    page = cbuf.shape[2]
    row = lax.broadcasted_iota(jnp.int32, (rows, LANES), 0)
    lane = lax.broadcasted_iota(jnp.int32, (rows, LANES), 1)

    def nope_scores(n_keys):
        blk = min(kblk, n_keys)
        inv = []
        for kb in range(n_keys // blk):
            kf = _dot(cc_ref[kb * blk:(kb + 1) * blk, :], wdh_ref[...])
            sq = kf * kf
            part = sq[:, :LANES]
            for t in range(1, sq.shape[1] // LANES):
                part = part + sq[:, LANES * t:LANES * (t + 1)]
            ssq = _dot_nt(rq_ref[...], part.astype(BF))
            inv.append(lax.rsqrt(ssq * (1.0 / nope) + EPS))
        inv = inv[0] if len(inv) == 1 else jnp.concatenate(inv, axis=1)
        return _dot_nt(qa_ref[...], cc_ref[:n_keys, :]) * inv

    a = q_ref[...].astype(F32).reshape(rows, HEAD_PAD)
    an = jnp.where(lane < nope, a, 0.0)
    head = row // t_new
    qbd = jnp.concatenate([jnp.where(head == h, an, 0.0) for h in range(n_heads)], axis=1).astype(BF)
    qa_ref[...] = _dot(qbd, wabs_ref[...]).astype(BF)
    qr_ref[...] = jnp.where(lane < rope, pltpu.roll(a, HEAD_PAD - nope, 1), 0.0).astype(BF)
    m_ref[...] = jnp.full_like(m_ref, NEG)
    l_ref[...] = jnp.zeros_like(l_ref)
    acc_ref[...] = jnp.zeros_like(acc_ref)
    cc_ref[:page, :] = jnp.concatenate(
        [c2_ref[...], jnp.zeros((page - t_new, c2_ref.shape[1]), F32)], axis=0).astype(BF)
    rn_ref[...] = jnp.zeros_like(rn_ref)
    rn_ref[:t_new, :rope] = r2_ref[...].astype(BF)
    keep = (lane < t_new) & (lane <= row % t_new)
    s = jnp.where(keep, nope_scores(page) + _dot_nt(qr_ref[...], rn_ref[...]), NEG)
    _softmax_step(s, cc_ref[:page, :], m_ref, l_ref, acc_ref)

    def compute(c, slot):
        for p in range(pps):
            cc_ref[page * p:page * (p + 1), :] = cbuf[slot, p].astype(BF)
            rc_ref[:, page * p:page * (p + 1)] = rbuf[slot, p].astype(BF)
        s = nope_scores(pps * page) + _dot(qr_ref[:, :rope], rc_ref[...])
        _softmax_step(s, cc_ref[...], m_ref, l_ref, acc_ref)

    _chunk_pipeline(pt_ref, (cc_hbm, cr_hbm), (cbuf, rbuf), sems, layer=layer, n_pages=n_pages,
                    pps=pps, nch=nch, reverse=False, compute=compute)

    olat = (acc_ref[...] / l_ref[...]).astype(BF)
    of = _dot(olat, wuv_ref[...])
    row2 = lax.broadcasted_iota(jnp.int32, of.shape, 0)
    lane2 = lax.broadcasted_iota(jnp.int32, of.shape, 1)
    om = jnp.where(lane2 // vdim == row2 // t_new, of, 0.0).astype(BF)
    o_ref[...] = _dot(sel_ref[...], om)


def _mla_decode(pt, q2, ckv2, kpe2, cache_c, cache_r, layer, w, *, t_new, pps):
    n_heads, ns, _ = q2.shape
    b2 = ns // t_new
    _, n_pool, page, kv_lora = cache_c.shape
    rope, nope, vdim = w["rope"], w["nope"], w["vdim"]
    n_pages = pt.shape[0] // b2
    nch = n_pages // pps
    assert nch % 2 == 0
    rows = n_heads * t_new
    rq = np.zeros((rows, LANES), np.float32)
    sel = np.zeros((t_new, rows), np.float32)
    for r in range(rows):
        rq[r, np.arange(LANES) % n_heads == r // t_new] = 1.0
        sel[r % t_new, r] = 1.0
    consts = [w["wuk_dh"], jnp.asarray(rq, BF), w["wabs"], w["wuv_flat"], jnp.asarray(sel, BF)]
    kern = functools.partial(_mla_decode_kernel, layer=layer, n_pages=n_pages, pps=pps, nch=nch, t_new=t_new,
                             n_heads=n_heads, nope=nope, rope=rope, vdim=vdim, kblk=pps * page)
    grid_spec = pltpu.PrefetchScalarGridSpec(
        num_scalar_prefetch=1,
        grid=(b2,),
        in_specs=[
            pl.BlockSpec((n_heads, t_new, HEAD_PAD), lambda b, ptr: (0, b, 0)),
            pl.BlockSpec((t_new, kv_lora), lambda b, ptr: (b, 0)),
            pl.BlockSpec((t_new, rope), lambda b, ptr: (b, 0)),
        ] + [pl.BlockSpec(x.shape, lambda b, ptr, nd=x.ndim: (0,) * nd) for x in consts]
        + [pl.BlockSpec(memory_space=pl.ANY)] * 2,
        out_specs=pl.BlockSpec((t_new, n_heads * vdim), lambda b, ptr: (b, 0)),
        scratch_shapes=[
            pltpu.VMEM((2, pps, page, kv_lora), F32), pltpu.VMEM((2, pps, rope, page), F32),
            pltpu.SemaphoreType.DMA((2, 2)),
            pltpu.VMEM((rows, kv_lora), BF), pltpu.VMEM((rows, LANES), BF),
            pltpu.VMEM((rows, 1), F32), pltpu.VMEM((rows, 1), F32), pltpu.VMEM((rows, kv_lora), F32),
            pltpu.VMEM((pps * page, kv_lora), BF), pltpu.VMEM((rope, pps * page), BF),
            pltpu.VMEM((page, LANES), BF),
        ],
    )
    return pl.pallas_call(
        kern, grid_spec=grid_spec,
        out_shape=jax.ShapeDtypeStruct((ns, n_heads * vdim), F32),
        compiler_params=_params(("arbitrary",)),
        name="mla_decode",
    )(pt, q2, ckv2, kpe2, *consts, cache_c, cache_r)


TM = 512
TQ = 256
TKC = 512
PPS = 32


def _ffn_chunk(dff):
    for parts in (2, 1, 4, 11, 22):
        if dff % parts == 0 and (dff // parts) % LANES == 0:
            return dff // parts
    return dff


def kernel(x_prompt, x_sample, cache_fox_k, cache_fox_v, cache_fox_logf, cache_mla_ckv, cache_mla_kpe, page_table, attn_norm, ffn_norm, fox_w_in, fox_b_f, fox_g_q, fox_g_k, fox_w_o, mla_w_in, mla_g_cq, mla_w_uq, mla_g_qn, mla_g_qr, mla_g_ckv, mla_g_kr, mla_w_ukv, mla_g_kn, mla_w_o, ffn_w_gu, ffn_w_down):
    b1, seq, d = x_prompt.shape
    b2, t_new, _ = x_sample.shape
    depth = attn_norm.shape[0]
    n_fox, n_pool, page, fox_kv, fox_hd = cache_fox_k.shape
    fox_heads = fox_b_f.shape[1]
    past = page_table.shape[1] * page
    dff = ffn_w_down.shape[1]
    tf = _ffn_chunk(dff)
    n_p, n_s = b1 * seq, b2 * t_new
    tm_s = min(TM, n_s)
    assert fox_hd == 64 and seq % TKC == 0 and seq % TM == 0 and n_s % tm_s == 0 and tm_s % t_new == 0
    assert page_table.shape[1] % PPS == 0 and page == LANES and t_new == SUBLANES

    hp = x_prompt.reshape(n_p, d)
    hs = x_sample.reshape(n_s, d)
    pt = page_table.reshape(-1).astype(jnp.int32)
    ck4 = jnp.transpose(cache_fox_k, (0, 1, 3, 4, 2)).reshape(n_fox, n_pool, fox_kv * fox_hd, page)
    cv4 = jnp.transpose(cache_fox_v, (0, 1, 3, 4, 2)).reshape(n_fox, n_pool, fox_kv * fox_hd, page)
    clf = jnp.transpose(cache_fox_logf, (0, 1, 3, 2))
    ckr = jnp.transpose(cache_mla_kpe, (0, 1, 3, 2))

    tri_p = jnp.asarray(np.tril(np.ones((TM, TM), np.float32)), BF)
    seq_id = np.arange(tm_s) // t_new
    tri_s = jnp.asarray(np.tril(np.ones((tm_s, tm_s), np.float32)) * (seq_id[:, None] == seq_id[None, :]), BF)
    rope_half = mla_g_kr.shape[1] // 2
    cos_p, sin_p = _rope_tables(jnp.arange(seq), rope_half)
    cos_s, sin_s = _rope_tables(past + jnp.arange(tm_s) % t_new, rope_half)

    fk_p, fv_p, fl_p, fk_s, fv_s, fl_s = [], [], [], [], [], []
    mc_p, mr_p, mc_s, mr_s = [], [], [], []
    for i in range(depth):
        j = i // 2
        if i % 2 == 0:
            w = _fox_weights(fox_w_in[j], fox_b_f[j], fox_g_q[j], fox_g_k[j], attn_norm[i], fox_heads, fox_kv, fox_hd)
            q, ka, vp, k, v, lf, _ = _fox_in(hp, w, tm=TM, tiles_per_seq=seq // TM, tri=tri_p)
            o_p = _flash_prompt(q, ka, vp, batch=b1, n_q=fox_heads // fox_kv, n_k=1, tq=TQ, tkc=TKC, vdim=fox_hd)
            q2, _, _, k2, v2, lf2, c2 = _fox_in(hs, w, tm=tm_s, tiles_per_seq=1, tri=tri_s)
            o_s = _fox_decode(pt, q2, c2, k2, v2, ck4, cv4, clf, j, t_new=t_new, pps=PPS)
            fk_p.append(k); fv_p.append(v); fl_p.append(lf)
            fk_s.append(k2); fv_s.append(v2); fl_s.append(lf2)
            wo = fox_w_o[j].astype(BF)
        else:
            w = _mla_weights(mla_w_in[j], mla_g_cq[j], mla_w_uq[j], mla_g_qn[j], mla_g_qr[j], mla_g_ckv[j],
                             mla_g_kr[j], mla_w_ukv[j], mla_g_kn[j], attn_norm[i])
            q, kk, vv, ckv, kpe = _mla_in(hp, cos_p, sin_p, w, tm=TM)
            o_p = _flash_prompt(q, kk, vv, batch=b1, n_q=2, n_k=2, tq=TKC, tkc=TKC, vdim=w["vdim"])
            q2, _, _, ckv2, kpe2 = _mla_in(hs, cos_s, sin_s, w, tm=tm_s)
            o_s = _mla_decode(pt, q2, ckv2, kpe2, cache_mla_ckv, ckr, j, w, t_new=t_new, pps=PPS)
            mc_p.append(ckv); mr_p.append(kpe)
            mc_s.append(ckv2); mr_s.append(kpe2)
            wo = mla_w_o[j].astype(BF)
        gf = ffn_norm[i].reshape(1, d)
        wg = ffn_w_gu[i, :, :dff].astype(BF)
        wu = ffn_w_gu[i, :, dff:].astype(BF)
        wd = ffn_w_down[i].astype(BF)
        hp = _out_ffn(hp, o_p, wo, gf, wg, wu, wd, tm=TM, tf=tf)
        hs = _out_ffn(hs, o_s, wo, gf, wg, wu, wd, tm=tm_s, tf=tf)

    def stk(xs, lead, tail):
        return jnp.stack(xs).reshape((len(xs),) + lead + tail)

    return (hp.reshape(b1, seq, d), hs.reshape(b2, t_new, d),
            stk(fk_p, (b1, seq), (fox_kv, fox_hd)), stk(fv_p, (b1, seq), (fox_kv, fox_hd)),
            stk(fl_p, (b1, seq), (fox_heads,)),
            stk(mc_p, (b1, seq), (mla_g_ckv.shape[1],)), stk(mr_p, (b1, seq), (mla_g_kr.shape[1],)),
            stk(fk_s, (b2, t_new), (fox_kv, fox_hd)), stk(fv_s, (b2, t_new), (fox_kv, fox_hd)),
            stk(fl_s, (b2, t_new), (fox_heads,)),
            stk(mc_s, (b2, t_new), (mla_g_ckv.shape[1],)), stk(mr_s, (b2, t_new), (mla_g_kr.shape[1],)))
```

```python
import functools

import numpy as np
import jax
import jax.numpy as jnp
from jax import lax
from jax.experimental import pallas as pl
from jax.experimental.pallas import tpu as pltpu

BF = jnp.bfloat16
F32 = jnp.float32
EPS = 1e-6
ROPE_THETA = 10000.0
LANES = 128
SUBLANES = 8
HEAD_PAD = LANES
NEG = -1e30
LOG2E = 1.4426950408889634
VMEM_LIMIT = 56 * 1024 * 1024
NT = (((1,), (1,)), ((), ()))


def _dot(a, b):
    return jnp.dot(a, b, preferred_element_type=F32)


def _dot_nt(a, b):
    return lax.dot_general(a, b, NT, preferred_element_type=F32)


def _split3(x):
    hi = x.astype(BF)
    r1 = x - hi.astype(F32)
    mid = r1.astype(BF)
    lo = (r1 - mid.astype(F32)).astype(BF)
    return hi, mid, lo


def _dot_exact_lhs(x, w):
    hi, mid, lo = _split3(x)
    return _dot(hi, w) + _dot(mid, w) + _dot(lo, w)


def _dot_exact_rhs(w, x):
    hi, mid, lo = _split3(x)
    return _dot(w, hi) + _dot(w, mid) + _dot(w, lo)


def _rms_rows(x, g):
    return x * lax.rsqrt(jnp.mean(x * x, axis=-1, keepdims=True) + EPS) * g


def _log_sigmoid(x):
    return jnp.minimum(x, 0.0) - jnp.log1p(jnp.exp(-jnp.abs(x)))


def _full(shape):
    n = len(shape)
    return pl.BlockSpec(shape, lambda *_: (0,) * n)


def _params(sem):
    return pltpu.CompilerParams(dimension_semantics=sem, vmem_limit_bytes=VMEM_LIMIT)


def _fox_in_kernel(x_ref, ga_ref, wq_ref, wk_ref, wv_ref, wkv_ref, wf_ref, bf_ref, gq_ref, gk_ref, gk2_ref,
                   g1_ref, g2_ref, tri_ref, pq_ref, pk_ref, oq_ref, ok_ref,
                   q_out, ka_out, vp_out, k_out, v_out, lf_out, c_out, carry_ref,
                   *, tiles_per_seq, n_heads, n_kv, n_gate, hd):
    i = pl.program_id(0)
    tm = x_ref.shape[0]
    xn = _rms_rows(x_ref[...], ga_ref[...]).astype(BF)
    lane = lax.broadcasted_iota(jnp.int32, (tm, LANES), 1)

    lf = jnp.where(lane < n_gate, _log_sigmoid(_dot(xn, wf_ref[...]) + bf_ref[...]), 0.0)
    lf_out[...] = lf[:, :n_gate]

    @pl.when(i % tiles_per_seq == 0)
    def _():
        carry_ref[...] = jnp.zeros_like(carry_ref)

    c = _dot_exact_rhs(tri_ref[...], lf) + carry_ref[...]
    carry_ref[...] = c[tm - 1:tm, :]
    c = c * LOG2E
    c_out[...] = c

    ch, cm, cl = _split3(c)
    cparts = (ch.astype(F32) + pltpu.roll(cm.astype(F32), 16, 1) + pltpu.roll(cl.astype(F32), 32, 1)).astype(BF)

    g1 = g1_ref[...]
    v_ones = jnp.where(lane >= hd, 1.0, 0.0)
    for j in range(n_heads // 2):
        sl = slice(2 * HEAD_PAD * j, 2 * HEAD_PAD * (j + 1))
        zq = _dot(xn, wq_ref[:, sl])
        aug = _dot(cparts, pq_ref[:, sl]) + oq_ref[:, sl]
        for u in range(2):
            z = zq[:, HEAD_PAD * u:HEAD_PAD * (u + 1)]
            ms = _dot((z * z).astype(BF), g1)
            qn = z * lax.rsqrt(ms + EPS) * gq_ref[...]
            q_out[2 * j + u] = (qn + aug[:, HEAD_PAD * u:HEAD_PAD * (u + 1)]).astype(BF)

    for j in range(n_kv // 2):
        sl = slice(2 * HEAD_PAD * j, 2 * HEAD_PAD * (j + 1))
        zk = _dot(xn, wk_ref[:, sl])
        zv = _dot(xn, wv_ref[:, sl])
        aug = _dot(cparts, pk_ref[:, sl]) + ok_ref[:, sl]
        for u in range(2):
            z = zk[:, HEAD_PAD * u:HEAD_PAD * (u + 1)]
            ms = _dot((z * z).astype(BF), g1)
            kn = z * lax.rsqrt(ms + EPS) * gk_ref[...]
            ka_out[2 * j + u] = (kn + aug[:, HEAD_PAD * u:HEAD_PAD * (u + 1)]).astype(BF)
            vp_out[2 * j + u] = (zv[:, HEAD_PAD * u:HEAD_PAD * (u + 1)] + v_ones).astype(BF)

    zkv = _dot(xn, wkv_ref[...])
    nk = wkv_ref.shape[1] // 2
    k = zkv[:, :nk]
    ms = _dot((k * k).astype(BF), g2_ref[...])
    k_out[...] = k * lax.rsqrt(ms + EPS) * gk2_ref[...]
    v_out[...] = zkv[:, nk:]


def _fox_in(x, w, *, tm, tiles_per_seq, tri):
    n, d = x.shape
    n_heads, n_kv, n_gate, hd = w["n_heads"], w["n_kv"], w["n_gate"], w["head_dim"]
    kern = functools.partial(_fox_in_kernel, tiles_per_seq=tiles_per_seq, n_heads=n_heads, n_kv=n_kv, n_gate=n_gate,
                             hd=hd)
    consts = [w["ga"], w["wq"], w["wk"], w["wv"], w["wkv"], w["wf"], w["bf"], w["gq"], w["gk"], w["gk2"],
              w["g1"], w["g2"], tri, w["pq"], w["pk"], w["oq"], w["ok"]]
    return pl.pallas_call(
        kern,
        grid=(n // tm,),
        in_specs=[pl.BlockSpec((tm, d), lambda i: (i, 0))] + [_full(c.shape) for c in consts],
        out_specs=[
            pl.BlockSpec((n_heads, tm, HEAD_PAD), lambda i: (0, i, 0)),
            pl.BlockSpec((n_kv, tm, HEAD_PAD), lambda i: (0, i, 0)),
            pl.BlockSpec((n_kv, tm, HEAD_PAD), lambda i: (0, i, 0)),
            pl.BlockSpec((tm, n_kv * hd), lambda i: (i, 0)),
            pl.BlockSpec((tm, n_kv * hd), lambda i: (i, 0)),
            pl.BlockSpec((tm, n_gate), lambda i: (i, 0)),
            pl.BlockSpec((tm, LANES), lambda i: (i, 0)),
        ],
        out_shape=[
            jax.ShapeDtypeStruct((n_heads, n, HEAD_PAD), BF),
            jax.ShapeDtypeStruct((n_kv, n, HEAD_PAD), BF),
            jax.ShapeDtypeStruct((n_kv, n, HEAD_PAD), BF),
            jax.ShapeDtypeStruct((n, n_kv * hd), F32),
            jax.ShapeDtypeStruct((n, n_kv * hd), F32),
            jax.ShapeDtypeStruct((n, n_gate), F32),
            jax.ShapeDtypeStruct((n, LANES), F32),
        ],
        scratch_shapes=[pltpu.VMEM((1, LANES), F32)],
        compiler_params=_params(("arbitrary",)),
        name="fox_in",
    )(x, *consts)


def _fox_weights(w_in, b_f, g_q, g_k, g_attn, n_heads, n_kv, hd):
    d = w_in.shape[0]
    nq, nk = n_heads * hd, n_kv * hd
    n_gate = n_heads
    grp = n_heads // n_kv
    scale = hd ** -0.5
    pad = HEAD_PAD - hd

    def heads_padded(wm, nh):
        return jnp.pad(wm.reshape(d, nh, hd), ((0, 0), (0, 0), (0, pad))).reshape(d, nh * HEAD_PAD).astype(BF)

    ck0, cq0 = hd, hd + 3 * grp
    pq = np.zeros((LANES, n_heads * HEAD_PAD), np.float32)
    oq = np.zeros((1, n_heads * HEAD_PAD), np.float32)
    pk = np.zeros((LANES, n_kv * HEAD_PAD), np.float32)
    ok = np.zeros((1, n_kv * HEAD_PAD), np.float32)
    for h in range(n_heads):
        kv, g = divmod(h, grp)
        for p in range(3):
            pq[16 * p + h, HEAD_PAD * h + cq0 + p] = 1.0
            oq[0, HEAD_PAD * h + ck0 + 3 * g + p] = 1.0
            pk[16 * p + h, HEAD_PAD * kv + ck0 + 3 * g + p] = -1.0
    for kv in range(n_kv):
        for p in range(3):
            ok[0, HEAD_PAD * kv + cq0 + p] = 1.0
    g1 = np.zeros((HEAD_PAD, HEAD_PAD), np.float32)
    g1[:hd, :hd] = 1.0 / hd
    g2 = np.kron(np.eye(n_kv, dtype=np.float32), np.full((hd, hd), 1.0 / hd, np.float32))
    return dict(
        n_heads=n_heads, n_kv=n_kv, n_gate=n_gate, head_dim=hd,
        ga=g_attn.reshape(1, d),
        wq=heads_padded(w_in[:, :nq], n_heads),
        wk=heads_padded(w_in[:, nq:nq + nk], n_kv),
        wv=heads_padded(w_in[:, nq + nk:nq + 2 * nk], n_kv),
        wkv=w_in[:, nq:nq + 2 * nk].astype(BF),
        wf=jnp.pad(w_in[:, nq + 2 * nk:], ((0, 0), (0, LANES - n_gate))).astype(BF),
        bf=jnp.pad(b_f, (0, LANES - n_gate)).reshape(1, LANES),
        gq=jnp.pad(g_q * (scale * LOG2E), (0, pad)).reshape(1, HEAD_PAD),
        gk=jnp.pad(g_k, (0, pad)).reshape(1, HEAD_PAD),
        gk2=jnp.tile(g_k, n_kv).reshape(1, nk),
        g1=jnp.asarray(g1, BF), g2=jnp.asarray(g2, BF),
        pq=jnp.asarray(pq, BF), pk=jnp.asarray(pk, BF), oq=jnp.asarray(oq), ok=jnp.asarray(ok),
    )


def _rope_lanes(x, cos, sin, lane, lo, half):
    nl = x.shape[-1]
    fwd = pltpu.roll(x, nl - half, 1)
    bwd = pltpu.roll(x, half, 1)
    first = (lane >= lo) & (lane < lo + half)
    second = (lane >= lo + half) & (lane < lo + 2 * half)
    rot = jnp.where(first, -fwd, jnp.where(second, bwd, 0.0))
    return jnp.where(first | second, x * cos + rot * sin, x)


def _mla_in_kernel(x_ref, cos_ref, sin_ref, ga_ref, win_ref, gcq_ref, gckv_ref, gkr_ref, wuq_ref, gq_ref,
                   wuk_ref, gk_ref, wuv_ref, g1_ref,
                   q_out, kk_out, vv_out, ckv_out, kpe_out,
                   *, n_heads, q_lora, kv_lora, nope, rope, vdim):
    tm = x_ref.shape[0]
    lane = lax.broadcasted_iota(jnp.int32, (tm, LANES), 1)
    cos, sin = cos_ref[...], sin_ref[...]
    xn = _rms_rows(x_ref[...], ga_ref[...]).astype(BF)
    z = _dot(xn, win_ref[...])
    cq = _rms_rows(z[:, :q_lora], gcq_ref[...]).astype(BF)
    ckv = _rms_rows(z[:, q_lora:q_lora + kv_lora], gckv_ref[...])
    ckv_out[...] = ckv
    ckv_b = ckv.astype(BF)
    kr = z[:, q_lora + kv_lora:]
    kr = kr * lax.rsqrt(jnp.sum(kr * kr, axis=-1, keepdims=True) * (1.0 / rope) + EPS) * gkr_ref[...]
    kpe = _rope_lanes(kr, cos, sin, lane, 0, rope // 2)
    kpe_out[...] = kpe[:, :rope]
    kpe_sh = pltpu.roll(kpe, nope, 1)

    g1 = g1_ref[...]
    v_ones = jnp.where(lane >= vdim, 1.0, 0.0)
    for j in range(n_heads // 2):
        sl = slice(2 * HEAD_PAD * j, 2 * HEAD_PAD * (j + 1))
        zq = _dot(cq, wuq_ref[:, sl])
        zk = _dot(ckv_b, wuk_ref[:, sl])
        zv = _dot(ckv_b, wuv_ref[:, sl])
        for u in range(2):
            bl = slice(HEAD_PAD * u, HEAD_PAD * (u + 1))
            q = zq[:, bl]
            q = q * lax.rsqrt(_dot((q * q).astype(BF), g1) + EPS) * gq_ref[...]
            q_out[2 * j + u] = _rope_lanes(q, cos, sin, lane, nope, rope // 2).astype(BF)
            k = zk[:, bl]
            k = k * lax.rsqrt(_dot((k * k).astype(BF), g1) + EPS) * gk_ref[...]
            kk_out[2 * j + u] = (k + kpe_sh).astype(BF)
            vv_out[2 * j + u] = (zv[:, bl] + v_ones).astype(BF)


def _mla_in(x, cos, sin, w, *, tm):
    n, d = x.shape
    n_heads, kv_lora, rope = w["n_heads"], w["kv_lora"], w["rope"]
    tbl_tiles = cos.shape[0] // tm
    kern = functools.partial(_mla_in_kernel, n_heads=n_heads, q_lora=w["q_lora"], kv_lora=kv_lora,
                             nope=w["nope"], rope=rope, vdim=w["vdim"])
    consts = [w["ga"], w["win"], w["gcq"], w["gckv"], w["gkr"], w["wuq"], w["gq"], w["wuk"], w["gk"], w["wuv"], w["g1"]]
    head_spec = pl.BlockSpec((n_heads, tm, HEAD_PAD), lambda i: (0, i, 0))
    head_shape = jax.ShapeDtypeStruct((n_heads, n, HEAD_PAD), BF)
    return pl.pallas_call(
        kern,
        grid=(n // tm,),
        in_specs=[pl.BlockSpec((tm, d), lambda i: (i, 0)),
                  pl.BlockSpec((tm, LANES), lambda i: (i % tbl_tiles, 0)),
                  pl.BlockSpec((tm, LANES), lambda i: (i % tbl_tiles, 0))] + [_full(c.shape) for c in consts],
        out_specs=[head_spec, head_spec, head_spec,
                   pl.BlockSpec((tm, kv_lora), lambda i: (i, 0)),
                   pl.BlockSpec((tm, rope), lambda i: (i, 0))],
        out_shape=[head_shape, head_shape, head_shape,
                   jax.ShapeDtypeStruct((n, kv_lora), F32),
                   jax.ShapeDtypeStruct((n, rope), F32)],
        compiler_params=_params(("parallel",)),
        name="mla_in",
    )(x, cos, sin, *consts)


def _mla_weights(w_in, g_cq, w_uq, g_qn, g_qr, g_ckv, g_kr, w_ukv, g_kn, g_attn):
    d = w_in.shape[0]
    q_lora, kv_lora, rope, nope = g_cq.shape[0], g_ckv.shape[0], g_kr.shape[0], g_qn.shape[0]
    n_heads = w_uq.shape[1]
    vdim = w_ukv.shape[2] - nope
    scale = (nope + rope) ** -0.5
    in_pad = -w_in.shape[1] % LANES
    w_uk, w_uv = w_ukv[..., :nope], w_ukv[..., nope:]

    def heads_padded(wm):
        r, nh, hd = wm.shape
        return jnp.pad(wm, ((0, 0), (0, 0), (0, HEAD_PAD - hd))).reshape(r, nh * HEAD_PAD).astype(BF)

    g1 = np.zeros((HEAD_PAD, HEAD_PAD), np.float32)
    g1[:nope, :nope] = 1.0 / nope
    g1[nope:nope + rope, nope:nope + rope] = 1.0 / rope
    wuk_dh = jnp.transpose(w_uk, (0, 2, 1)).reshape(kv_lora, nope * n_heads).astype(BF)
    wabs = jnp.transpose(w_uk * g_kn[None, None, :], (1, 2, 0))
    wabs = jnp.pad(wabs, ((0, 0), (0, HEAD_PAD - nope), (0, 0))).reshape(n_heads * HEAD_PAD, kv_lora).astype(BF)
    return dict(
        n_heads=n_heads, q_lora=q_lora, kv_lora=kv_lora, rope=rope, nope=nope, vdim=vdim,
        ga=g_attn.reshape(1, d),
        win=jnp.pad(w_in, ((0, 0), (0, in_pad))).astype(BF),
        gcq=g_cq.reshape(1, -1), gckv=g_ckv.reshape(1, -1),
        gkr=jnp.pad(g_kr, (0, LANES - rope)).reshape(1, LANES),
        wuq=heads_padded(w_uq),
        gq=jnp.pad(jnp.concatenate([g_qn, g_qr]) * (scale * LOG2E), (0, HEAD_PAD - nope - rope)).reshape(1, HEAD_PAD),
        wuk=heads_padded(w_uk), gk=jnp.pad(g_kn, (0, HEAD_PAD - nope)).reshape(1, HEAD_PAD),
        wuv=heads_padded(w_uv), g1=jnp.asarray(g1, BF),
        wuk_dh=wuk_dh, wabs=wabs, wuv_flat=w_uv.reshape(kv_lora, n_heads * vdim).astype(BF),
    )


def _rope_tables(pos, half):
    inv = ROPE_THETA ** (-jnp.arange(half, dtype=F32) / half)
    ang = pos.astype(F32)[:, None] * inv[None, :]
    reps = LANES // half
    return jnp.tile(jnp.cos(ang), (1, reps)), jnp.tile(jnp.sin(ang), (1, reps))


def _flash_kernel(q_ref, k_ref, v_ref, o_ref, s_ref, m_ref, acc_ref, *, n_q, n_k, vdim, tkc):
    qi = pl.program_id(2)
    tq = q_ref.shape[1]
    diag = (qi * tq) // tkc
    off = qi * tq - diag * tkc
    lane_tiles = tkc // LANES

    m_ref[...] = jnp.full_like(m_ref, NEG)
    acc_ref[...] = jnp.zeros_like(acc_ref)

    def lane_max(s):
        m = s[:, :LANES]
        for t in range(1, lane_tiles):
            m = jnp.maximum(m, s[:, LANES * t:LANES * (t + 1)])
        return m

    def score_chunk(kc, keep):
        k0 = pl.multiple_of(kc * tkc, tkc)
        for r in range(n_q):
            s = _dot_nt(q_ref[r], k_ref[r * n_k // n_q, pl.ds(k0, tkc), :])
            if keep is not None:
                s = jnp.where(keep, s, NEG)
            s_ref[r, kc] = s
            m_ref[r] = jnp.maximum(m_ref[r], lane_max(s))

    def pass_a(kc, carry):
        score_chunk(kc, None)
        return carry

    lax.fori_loop(0, diag, pass_a, 0)
    row = lax.broadcasted_iota(jnp.int32, (tq, tkc), 0)
    col = lax.broadcasted_iota(jnp.int32, (tq, tkc), 1)
    score_chunk(diag, col <= row + off)

    for r in range(n_q):
        m_ref[r] = jnp.broadcast_to(jnp.max(m_ref[r], axis=-1, keepdims=True), (tq, LANES))

    def pass_b(kc, carry):
        k0 = pl.multiple_of(kc * tkc, tkc)
        for r in range(n_q):
            m = m_ref[r]
            p = jnp.exp2(s_ref[r, kc] - jnp.concatenate([m] * lane_tiles, axis=1)).astype(BF)
            acc_ref[r] += _dot(p, v_ref[r * n_k // n_q, pl.ds(k0, tkc), :])
        return carry

    lax.fori_loop(0, diag + 1, pass_b, 0)

    lane = lax.broadcasted_iota(jnp.int32, (tq, LANES), 1)

    def normalised(r):
        a = acc_ref[r]
        return jnp.where(lane < vdim, a / jnp.where(lane < vdim, pltpu.roll(a, vdim, 1), 1.0), 0.0)

    for j in range(n_q // 2):
        o_ref[:, LANES * j:LANES * (j + 1)] = (
            normalised(2 * j) + pltpu.roll(normalised(2 * j + 1), vdim, 1)).astype(o_ref.dtype)


def _flash_prompt(q, k, v, *, batch, n_q, n_k, tq, tkc, vdim):
    hq, n, _ = q.shape
    seq = n // batch
    nb = seq // tq
    assert tkc % tq == 0 and seq % tkc == 0 and 2 * vdim == LANES
    kern = functools.partial(_flash_kernel, n_q=n_q, n_k=n_k, vdim=vdim, tkc=tkc)
    return pl.pallas_call(
        kern,
        grid=(batch, hq // n_q, nb),
        in_specs=[
            pl.BlockSpec((n_q, tq, HEAD_PAD), lambda b, j, i: (j, b * nb + i, 0)),
            pl.BlockSpec((n_k, seq, HEAD_PAD), lambda b, j, i: (j, b, 0)),
            pl.BlockSpec((n_k, seq, HEAD_PAD), lambda b, j, i: (j, b, 0)),
        ],
        out_specs=pl.BlockSpec((tq, n_q * vdim), lambda b, j, i: (b * nb + i, j)),
        out_shape=jax.ShapeDtypeStruct((n, hq * vdim), BF),
        scratch_shapes=[pltpu.VMEM((n_q, seq // tkc, tq, tkc), F32), pltpu.VMEM((n_q, tq, LANES), F32),
                        pltpu.VMEM((n_q, tq, HEAD_PAD), F32)],
        compiler_params=_params(("parallel", "parallel", "arbitrary")),
        name="flash_prompt",
    )(q, k, v)


def _out_ffn_kernel(h_ref, o_ref, wo_ref, gf_ref, wg_ref, wu_ref, wd_ref, out_ref, h1_ref, xn_ref, acc_ref):
    f = pl.program_id(1)

    @pl.when(f == 0)
    def _():
        h1 = h_ref[...] + _dot(o_ref[...].astype(BF), wo_ref[...])
        h1_ref[...] = h1
        xn_ref[...] = _rms_rows(h1, gf_ref[...]).astype(BF)
        acc_ref[...] = jnp.zeros_like(acc_ref)

    xn = xn_ref[...]
    g = _dot(xn, wg_ref[...])
    u = _dot(xn, wu_ref[...])
    acc_ref[...] += _dot((g * jax.nn.sigmoid(g) * u).astype(BF), wd_ref[...])

    @pl.when(f == pl.num_programs(1) - 1)
    def _():
        out_ref[...] = h1_ref[...] + acc_ref[...]


def _out_ffn(h, o, wo, gf, wg, wu, wd, *, tm, tf):
    n, d = h.shape
    do = o.shape[1]
    dff = wg.shape[1]
    return pl.pallas_call(
        _out_ffn_kernel,
        grid=(n // tm, dff // tf),
        in_specs=[
            pl.BlockSpec((tm, d), lambda i, f: (i, 0)),
            pl.BlockSpec((tm, do), lambda i, f: (i, 0)),
            pl.BlockSpec((do, d), lambda i, f: (0, 0)),
            pl.BlockSpec((1, d), lambda i, f: (0, 0)),
            pl.BlockSpec((d, tf), lambda i, f: (0, f)),
            pl.BlockSpec((d, tf), lambda i, f: (0, f)),
            pl.BlockSpec((tf, d), lambda i, f: (f, 0)),
        ],
        out_specs=pl.BlockSpec((tm, d), lambda i, f: (i, 0)),
        out_shape=jax.ShapeDtypeStruct((n, d), F32),
        scratch_shapes=[pltpu.VMEM((tm, d), F32), pltpu.VMEM((tm, d), BF), pltpu.VMEM((tm, d), F32)],
        compiler_params=_params(("parallel", "arbitrary")),
        name="out_ffn",
    )(h, o, wo, gf, wg, wu, wd)


def _softmax_step(s, v, m_ref, l_ref, acc_ref, v_is_transposed=False):
    m_prev = m_ref[...]
    m_new = jnp.maximum(m_prev, jnp.max(s, axis=-1, keepdims=True))
    alpha = jnp.exp2(m_prev - m_new)
    p = jnp.exp2(s - m_new)
    l_ref[...] = alpha * l_ref[...] + jnp.sum(p, axis=-1, keepdims=True)
    p = p.astype(BF)
    acc_ref[...] = alpha * acc_ref[...] + (_dot_nt(p, v) if v_is_transposed else _dot(p, v))
    m_ref[...] = m_new


def _rows_from_heads(x, t_new):
    return jnp.concatenate([jnp.broadcast_to(x[h:h + 1], (t_new, x.shape[1])) for h in range(x.shape[0])], axis=0)


def _chunk_pipeline(pt_ref, srcs, bufs, sems, *, layer, n_pages, pps, nch, reverse, compute):
    b = pl.program_id(0)
    n_seq = pl.num_programs(0)

    def chunk_copies(seq, c, slot):
        first = ((nch - 1 - c) if reverse else c) * pps
        out = []
        for p in range(pps):
            pid = pt_ref[seq * n_pages + first + p]
            for i, (src, buf) in enumerate(zip(srcs, bufs)):
                out.append(pltpu.make_async_copy(src.at[layer, pid], buf.at[slot, p], sems.at[i, slot]))
        return out

    @pl.when(b == 0)
    def _():
        for cp in chunk_copies(b, 0, 0):
            cp.start()

    for c in range(nch):
        slot = c % 2
        if c + 1 < nch:
            for cp in chunk_copies(b, c + 1, 1 - slot):
                cp.start()
        else:
            @pl.when(b + 1 < n_seq)
            def _():
                for cp in chunk_copies(b + 1, 0, 1 - slot):
                    cp.start()
        for cp in chunk_copies(b, c, slot):
            cp.wait()
        compute(c, slot)


def _fox_decode_kernel(pt_ref, q_ref, c_ref, k2_ref, v2_ref, u_ref, pg_ref, sg_ref, ck_hbm, cv_hbm, cl_hbm,
                       o_ref, kbuf, vbuf, lbuf, sems, qbd_ref, m_ref, l_ref, acc_ref, kc_ref, vc_ref,
                       *, layer, n_pages, pps, nch, t_new, n_heads, n_kv, hd):
    rows = n_heads * t_new
    page = kbuf.shape[3]
    kvw = n_kv * hd
    row = lax.broadcasted_iota(jnp.int32, (rows, LANES), 0)
    lane = lax.broadcasted_iota(jnp.int32, (rows, LANES), 1)

    a = q_ref[...].astype(F32).reshape(rows, HEAD_PAD)
    a = jnp.where(lane < hd, a, 0.0)
    kvh = row // (rows // n_kv)
    for blk in range(kvw // LANES):
        parts = 0.0
        for u in range(LANES // hd):
            parts = parts + jnp.where(kvh == blk * (LANES // hd) + u, pltpu.roll(a, hd * u, 1) if u else a, 0.0)
        qbd_ref[:, LANES * blk:LANES * (blk + 1)] = parts.astype(BF)
    cn = c_ref[...]
    cn_rows = jnp.concatenate([cn] * n_heads, axis=0)
    cn_row = jnp.sum(jnp.where(lane == row // t_new, cn_rows, 0.0), axis=-1, keepdims=True)
    m_ref[...] = jnp.full_like(m_ref, NEG)
    l_ref[...] = jnp.zeros_like(l_ref)
    acc_ref[...] = jnp.zeros_like(acc_ref)
    zpad = jnp.zeros((page - t_new, kvw), F32)
    kn = jnp.concatenate([k2_ref[...], zpad], axis=0).astype(BF)
    vn = jnp.concatenate([v2_ref[...], zpad], axis=0).astype(BF)
    cn_pad = jnp.concatenate([cn, jnp.zeros((page - t_new, LANES), F32)], axis=0)
    cn_t = cn_pad.T[:n_heads]
    bias = cn_row - _rows_from_heads(cn_t, t_new)
    keep = (lane < t_new) & (lane <= row % t_new)
    s = jnp.where(keep, _dot_nt(qbd_ref[...], kn) + bias, NEG)
    _softmax_step(s, vn, m_ref, l_ref, acc_ref)

    u_mat = u_ref[...]
    carry = [jnp.zeros((n_heads, 1), F32)]

    def compute(c, slot):
        lf_t = jnp.concatenate([lbuf[slot, p] for p in range(pps)], axis=0)
        within = _dot_exact_lhs(lf_t, u_mat)
        total = jnp.sum(lf_t, axis=-1, keepdims=True)
        sfx = [None] * pps
        for p in reversed(range(pps)):
            sfx[p] = within[n_heads * p:n_heads * (p + 1)] + carry[0]
            carry[0] = carry[0] + total[n_heads * p:n_heads * (p + 1)]
            kc_ref[:, page * p:page * (p + 1)] = kbuf[slot, p].astype(BF)
            vc_ref[:, page * p:page * (p + 1)] = vbuf[slot, p].astype(BF)
        bias = _rows_from_heads(jnp.concatenate(sfx, axis=1) * LOG2E, t_new) + cn_row
        s = _dot(qbd_ref[...], kc_ref[...]) + bias
        _softmax_step(s, vc_ref[...], m_ref, l_ref, acc_ref, v_is_transposed=True)

    _chunk_pipeline(pt_ref, (ck_hbm, cv_hbm, cl_hbm), (kbuf, vbuf, lbuf), sems, layer=layer, n_pages=n_pages,
                    pps=pps, nch=nch, reverse=True, compute=compute)

    o = acc_ref[...] / l_ref[...]
    row2 = lax.broadcasted_iota(jnp.int32, (rows, kvw), 0)
    lane2 = lax.broadcasted_iota(jnp.int32, (rows, kvw), 1)
    om = jnp.where(lane2 // hd == row2 // (rows // n_kv), o, 0.0).astype(BF)
    out = jnp.zeros(o_ref.shape, F32)
    for g in range(n_heads // n_kv):
        out = out + _dot(sg_ref[g], _dot(om, pg_ref[g]).astype(BF))
    o_ref[...] = out


def _fox_decode(pt, q2, c2, k2, v2, cache_k, cache_v, cache_lf, layer, *, t_new, pps):
    n_heads, ns, _ = q2.shape
    b2 = ns // t_new
    _, n_pool, kvw, page = cache_k.shape
    n_gate = cache_lf.shape[2]
    n_pages = pt.shape[0] // b2
    nch = n_pages // pps
    assert nch % 2 == 0
    hd = 64
    n_kv = kvw // hd
    grp = n_heads // n_kv
    rows = n_heads * t_new
    u_mat = jnp.asarray(np.tril(np.ones((page, page), np.float32), -1), BF)
    pg = np.zeros((grp, kvw, n_heads * hd), np.float32)
    sg = np.zeros((grp, t_new, rows), np.float32)
    for h in range(n_heads):
        kv, g = divmod(h, grp)
        for dd in range(hd):
            pg[g, kv * hd + dd, h * hd + dd] = 1.0
        for t in range(t_new):
            sg[g, t, h * t_new + t] = 1.0
    consts = [u_mat, jnp.asarray(pg, BF), jnp.asarray(sg, BF)]
    kern = functools.partial(_fox_decode_kernel, layer=layer, n_pages=n_pages, pps=pps, nch=nch, t_new=t_new,
                             n_heads=n_heads, n_kv=n_kv, hd=hd)
    grid_spec = pltpu.PrefetchScalarGridSpec(
        num_scalar_prefetch=1,
        grid=(b2,),
        in_specs=[
            pl.BlockSpec((n_heads, t_new, HEAD_PAD), lambda b, ptr: (0, b, 0)),
            pl.BlockSpec((t_new, LANES), lambda b, ptr: (b, 0)),
            pl.BlockSpec((t_new, kvw), lambda b, ptr: (b, 0)),
            pl.BlockSpec((t_new, kvw), lambda b, ptr: (b, 0)),
        ] + [pl.BlockSpec(x.shape, lambda b, ptr, nd=x.ndim: (0,) * nd) for x in consts]
        + [pl.BlockSpec(memory_space=pl.ANY)] * 3,
        out_specs=pl.BlockSpec((t_new, n_heads * hd), lambda b, ptr: (b, 0)),
        scratch_shapes=[
            pltpu.VMEM((2, pps, kvw, page), F32), pltpu.VMEM((2, pps, kvw, page), F32),
            pltpu.VMEM((2, pps, n_gate, page), F32), pltpu.SemaphoreType.DMA((3, 2)),
            pltpu.VMEM((rows, kvw), BF),
            pltpu.VMEM((rows, 1), F32), pltpu.VMEM((rows, 1), F32), pltpu.VMEM((rows, kvw), F32),
            pltpu.VMEM((kvw, pps * page), BF), pltpu.VMEM((kvw, pps * page), BF),
        ],
    )
    return pl.pallas_call(
        kern, grid_spec=grid_spec,
        out_shape=jax.ShapeDtypeStruct((ns, n_heads * hd), F32),
        compiler_params=_params(("arbitrary",)),
        name="fox_decode",
    )(pt, q2, c2, k2, v2, *consts, cache_k, cache_v, cache_lf)


def _mla_decode_kernel(pt_ref, q_ref, c2_ref, r2_ref, wdh_ref, rq_ref, wabs_ref, wuv_ref, sel_ref, cc_hbm, cr_hbm,
                       o_ref, cbuf, rbuf, sems, qa_ref, qr_ref, m_ref, l_ref, acc_ref, cc_ref, rc_ref, rn_ref,
                       *, layer, n_pages, pps, nch, t_new, n_heads, nope, rope, vdim, kblk):
    rows = n_heads * t_new
    page = cbuf.shape[2]
    row = lax.broadcasted_iota(jnp.int32, (rows, LANES), 0)
    lane = lax.broadcasted_iota(jnp.int32, (rows, LANES), 1)

    def nope_scores(n_keys):
        blk = min(kblk, n_keys)
        inv = []
        for kb in range(n_keys // blk):
            kf = _dot(cc_ref[kb * blk:(kb + 1) * blk, :], wdh_ref[...])
            sq = kf * kf
            part = sq[:, :LANES]
            for t in range(1, sq.shape[1] // LANES):
                part = part + sq[:, LANES * t:LANES * (t + 1)]
            ssq = _dot_nt(rq_ref[...], part.astype(BF))
            inv.append(lax.rsqrt(ssq * (1.0 / nope) + EPS))
        inv = inv[0] if len(inv) == 1 else jnp.concatenate(inv, axis=1)
        return _dot_nt(qa_ref[...], cc_ref[:n_keys, :]) * inv

    a = q_ref[...].astype(F32).reshape(rows, HEAD_PAD)
    an = jnp.where(lane < nope, a, 0.0)
    head = row // t_new
    qbd = jnp.concatenate([jnp.where(head == h, an, 0.0) for h in range(n_heads)], axis=1).astype(BF)
    qa_ref[...] = _dot(qbd, wabs_ref[...]).astype(BF)
    qr_ref[...] = jnp.where(lane < rope, pltpu.roll(a, HEAD_PAD - nope, 1), 0.0).astype(BF)
    m_ref[...] = jnp.full_like(m_ref, NEG)
    l_ref[...] = jnp.zeros_like(l_ref)
    acc_ref[...] = jnp.zeros_like(acc_ref)
    cc_ref[:page, :] = jnp.concatenate(
        [c2_ref[...], jnp.zeros((page - t_new, c2_ref.shape[1]), F32)], axis=0).astype(BF)
    rn_ref[...] = jnp.zeros_like(rn_ref)
    rn_ref[:t_new, :rope] = r2_ref[...].astype(BF)
    keep = (lane < t_new) & (lane <= row % t_new)
    s = jnp.where(keep, nope_scores(page) + _dot_nt(qr_ref[...], rn_ref[...]), NEG)
    _softmax_step(s, cc_ref[:page, :], m_ref, l_ref, acc_ref)

    def compute(c, slot):
        for p in range(pps):
            cc_ref[page * p:page * (p + 1), :] = cbuf[slot, p].astype(BF)
            rc_ref[:, page * p:page * (p + 1)] = rbuf[slot, p].astype(BF)
        s = nope_scores(pps * page) + _dot(qr_ref[:, :rope], rc_ref[...])
        _softmax_step(s, cc_ref[...], m_ref, l_ref, acc_ref)

    _chunk_pipeline(pt_ref, (cc_hbm, cr_hbm), (cbuf, rbuf), sems, layer=layer, n_pages=n_pages,
                    pps=pps, nch=nch, reverse=False, compute=compute)

    olat = (acc_ref[...] / l_ref[...]).astype(BF)
    of = _dot(olat, wuv_ref[...])
    row2 = lax.broadcasted_iota(jnp.int32, of.shape, 0)
    lane2 = lax.broadcasted_iota(jnp.int32, of.shape, 1)
    om = jnp.where(lane2 // vdim == row2 // t_new, of, 0.0).astype(BF)
    o_ref[...] = _dot(sel_ref[...], om)


def _mla_decode(pt, q2, ckv2, kpe2, cache_c, cache_r, layer, w, *, t_new, pps):
    n_heads, ns, _ = q2.shape
    b2 = ns // t_new
    _, n_pool, page, kv_lora = cache_c.shape
    rope, nope, vdim = w["rope"], w["nope"], w["vdim"]
    n_pages = pt.shape[0] // b2
    nch = n_pages // pps
    assert nch % 2 == 0
    rows = n_heads * t_new
    rq = np.zeros((rows, LANES), np.float32)
    sel = np.zeros((t_new, rows), np.float32)
    for r in range(rows):
        rq[r, np.arange(LANES) % n_heads == r // t_new] = 1.0
        sel[r % t_new, r] = 1.0
    consts = [w["wuk_dh"], jnp.asarray(rq, BF), w["wabs"], w["wuv_flat"], jnp.asarray(sel, BF)]
    kern = functools.partial(_mla_decode_kernel, layer=layer, n_pages=n_pages, pps=pps, nch=nch, t_new=t_new,
                             n_heads=n_heads, nope=nope, rope=rope, vdim=vdim, kblk=pps * page)
    grid_spec = pltpu.PrefetchScalarGridSpec(
        num_scalar_prefetch=1,
        grid=(b2,),
        in_specs=[
            pl.BlockSpec((n_heads, t_new, HEAD_PAD), lambda b, ptr: (0, b, 0)),
            pl.BlockSpec((t_new, kv_lora), lambda b, ptr: (b, 0)),
            pl.BlockSpec((t_new, rope), lambda b, ptr: (b, 0)),
        ] + [pl.BlockSpec(x.shape, lambda b, ptr, nd=x.ndim: (0,) * nd) for x in consts]
        + [pl.BlockSpec(memory_space=pl.ANY)] * 2,
        out_specs=pl.BlockSpec((t_new, n_heads * vdim), lambda b, ptr: (b, 0)),
        scratch_shapes=[
            pltpu.VMEM((2, pps, page, kv_lora), F32), pltpu.VMEM((2, pps, rope, page), F32),
            pltpu.SemaphoreType.DMA((2, 2)),
            pltpu.VMEM((rows, kv_lora), BF), pltpu.VMEM((rows, LANES), BF),
            pltpu.VMEM((rows, 1), F32), pltpu.VMEM((rows, 1), F32), pltpu.VMEM((rows, kv_lora), F32),
            pltpu.VMEM((pps * page, kv_lora), BF), pltpu.VMEM((rope, pps * page), BF),
            pltpu.VMEM((page, LANES), BF),
        ],
    )
    return pl.pallas_call(
        kern, grid_spec=grid_spec,
        out_shape=jax.ShapeDtypeStruct((ns, n_heads * vdim), F32),
        compiler_params=_params(("arbitrary",)),
        name="mla_decode",
    )(pt, q2, ckv2, kpe2, *consts, cache_c, cache_r)


TM = 512
TQ = 256
TQ_MLA = 512
TKC = 1024
PPS = 32


def _ffn_chunk(dff):
    for parts in (2, 1, 4, 11, 22):
        if dff % parts == 0 and (dff // parts) % LANES == 0:
            return dff // parts
    return dff


def kernel(x_prompt, x_sample, cache_fox_k, cache_fox_v, cache_fox_logf, cache_mla_ckv, cache_mla_kpe, page_table, attn_norm, ffn_norm, fox_w_in, fox_b_f, fox_g_q, fox_g_k, fox_w_o, mla_w_in, mla_g_cq, mla_w_uq, mla_g_qn, mla_g_qr, mla_g_ckv, mla_g_kr, mla_w_ukv, mla_g_kn, mla_w_o, ffn_w_gu, ffn_w_down):
    b1, seq, d = x_prompt.shape
    b2, t_new, _ = x_sample.shape
    depth = attn_norm.shape[0]
    n_fox, n_pool, page, fox_kv, fox_hd = cache_fox_k.shape
    fox_heads = fox_b_f.shape[1]
    past = page_table.shape[1] * page
    dff = ffn_w_down.shape[1]
    tf = _ffn_chunk(dff)
    n_p, n_s = b1 * seq, b2 * t_new
    tm_s = min(TM, n_s)
    assert fox_hd == 64 and seq % TKC == 0 and seq % TM == 0 and n_s % tm_s == 0 and tm_s % t_new == 0
    assert page_table.shape[1] % PPS == 0 and page == LANES and t_new == SUBLANES

    hp = x_prompt.reshape(n_p, d)
    hs = x_sample.reshape(n_s, d)
    pt = page_table.reshape(-1).astype(jnp.int32)
    ck4 = jnp.transpose(cache_fox_k, (0, 1, 3, 4, 2)).reshape(n_fox, n_pool, fox_kv * fox_hd, page)
    cv4 = jnp.transpose(cache_fox_v, (0, 1, 3, 4, 2)).reshape(n_fox, n_pool, fox_kv * fox_hd, page)
    clf = jnp.transpose(cache_fox_logf, (0, 1, 3, 2))
    ckr = jnp.transpose(cache_mla_kpe, (0, 1, 3, 2))

    tri_p = jnp.asarray(np.tril(np.ones((TM, TM), np.float32)), BF)
    seq_id = np.arange(tm_s) // t_new
    tri_s = jnp.asarray(np.tril(np.ones((tm_s, tm_s), np.float32)) * (seq_id[:, None] == seq_id[None, :]), BF)
    rope_half = mla_g_kr.shape[1] // 2
    cos_p, sin_p = _rope_tables(jnp.arange(seq), rope_half)
    cos_s, sin_s = _rope_tables(past + jnp.arange(tm_s) % t_new, rope_half)

    fk_p, fv_p, fl_p, fk_s, fv_s, fl_s = [], [], [], [], [], []
    mc_p, mr_p, mc_s, mr_s = [], [], [], []
    for i in range(depth):
        j = i // 2
        if i % 2 == 0:
            w = _fox_weights(fox_w_in[j], fox_b_f[j], fox_g_q[j], fox_g_k[j], attn_norm[i], fox_heads, fox_kv, fox_hd)
            q, ka, vp, k, v, lf, _ = _fox_in(hp, w, tm=TM, tiles_per_seq=seq // TM, tri=tri_p)
            o_p = _flash_prompt(q, ka, vp, batch=b1, n_q=fox_heads // fox_kv, n_k=1, tq=TQ, tkc=TKC, vdim=fox_hd)
            q2, _, _, k2, v2, lf2, c2 = _fox_in(hs, w, tm=tm_s, tiles_per_seq=1, tri=tri_s)
            o_s = _fox_decode(pt, q2, c2, k2, v2, ck4, cv4, clf, j, t_new=t_new, pps=PPS)
            fk_p.append(k); fv_p.append(v); fl_p.append(lf)
            fk_s.append(k2); fv_s.append(v2); fl_s.append(lf2)
            wo = fox_w_o[j].astype(BF)
        else:
            w = _mla_weights(mla_w_in[j], mla_g_cq[j], mla_w_uq[j], mla_g_qn[j], mla_g_qr[j], mla_g_ckv[j],
                             mla_g_kr[j], mla_w_ukv[j], mla_g_kn[j], attn_norm[i])
            q, kk, vv, ckv, kpe = _mla_in(hp, cos_p, sin_p, w, tm=TM)
            o_p = _flash_prompt(q, kk, vv, batch=b1, n_q=2, n_k=2, tq=TQ_MLA, tkc=TKC, vdim=w["vdim"])
            q2, _, _, ckv2, kpe2 = _mla_in(hs, cos_s, sin_s, w, tm=tm_s)
            o_s = _mla_decode(pt, q2, ckv2, kpe2, cache_mla_ckv, ckr, j, w, t_new=t_new, pps=PPS)
            mc_p.append(ckv); mr_p.append(kpe)
            mc_s.append(ckv2); mr_s.append(kpe2)
            wo = mla_w_o[j].astype(BF)
        gf = ffn_norm[i].reshape(1, d)
        wg = ffn_w_gu[i, :, :dff].astype(BF)
        wu = ffn_w_gu[i, :, dff:].astype(BF)
        wd = ffn_w_down[i].astype(BF)
        hp = _out_ffn(hp, o_p, wo, gf, wg, wu, wd, tm=TM, tf=tf)
        hs = _out_ffn(hs, o_s, wo, gf, wg, wu, wd, tm=tm_s, tf=tf)

    def stk(xs, lead, tail):
        return jnp.stack(xs).reshape((len(xs),) + lead + tail)

    return (hp.reshape(b1, seq, d), hs.reshape(b2, t_new, d),
            stk(fk_p, (b1, seq), (fox_kv, fox_hd)), stk(fv_p, (b1, seq), (fox_kv, fox_hd)),
            stk(fl_p, (b1, seq), (fox_heads,)),
            stk(mc_p, (b1, seq), (mla_g_ckv.shape[1],)), stk(mr_p, (b1, seq), (mla_g_kr.shape[1],)),
            stk(fk_s, (b2, t_new), (fox_kv, fox_hd)), stk(fv_s, (b2, t_new), (fox_kv, fox_hd)),
            stk(fl_s, (b2, t_new), (fox_heads,)),
            stk(mc_s, (b2, t_new), (mla_g_ckv.shape[1],)), stk(mr_s, (b2, t_new), (mla_g_kr.shape[1],)))
```

```python
import functools

import numpy as np
import jax
import jax.numpy as jnp
from jax import lax
from jax.experimental import pallas as pl
from jax.experimental.pallas import tpu as pltpu

BF = jnp.bfloat16
F32 = jnp.float32
EPS = 1e-6
ROPE_THETA = 10000.0
LANES = 128
SUBLANES = 8
HEAD_PAD = LANES
NEG = -1e30
LOG2E = 1.4426950408889634
VMEM_LIMIT = 56 * 1024 * 1024
NT = (((1,), (1,)), ((), ()))


def _dot(a, b):
    return jnp.dot(a, b, preferred_element_type=F32)


def _dot_nt(a, b):
    return lax.dot_general(a, b, NT, preferred_element_type=F32)


def _split3(x):
    hi = x.astype(BF)
    r1 = x - hi.astype(F32)
    mid = r1.astype(BF)
    lo = (r1 - mid.astype(F32)).astype(BF)
    return hi, mid, lo


def _dot_exact_lhs(x, w):
    hi, mid, lo = _split3(x)
    return _dot(hi, w) + _dot(mid, w) + _dot(lo, w)


def _dot_exact_rhs(w, x):
    hi, mid, lo = _split3(x)
    return _dot(w, hi) + _dot(w, mid) + _dot(w, lo)


def _rms_rows(x, g):
    return x * lax.rsqrt(jnp.mean(x * x, axis=-1, keepdims=True) + EPS) * g


def _log_sigmoid(x):
    return jnp.minimum(x, 0.0) - jnp.log1p(jnp.exp(-jnp.abs(x)))


def _full(shape):
    n = len(shape)
    return pl.BlockSpec(shape, lambda *_: (0,) * n)


def _params(sem):
    return pltpu.CompilerParams(dimension_semantics=sem, vmem_limit_bytes=VMEM_LIMIT)


def _fox_in_kernel(x_ref, ga_ref, wq_ref, wk_ref, wv_ref, wkv_ref, wf_ref, bf_ref, gq_ref, gk_ref, gk2_ref,
                   g1_ref, g2_ref, tri_ref, pq_ref, pk_ref, oq_ref, ok_ref,
                   q_out, ka_out, vp_out, k_out, v_out, lf_out, c_out, carry_ref,
                   *, tiles_per_seq, n_heads, n_kv, n_gate, hd):
    i = pl.program_id(0)
    tm = x_ref.shape[0]
    xn = _rms_rows(x_ref[...], ga_ref[...]).astype(BF)
    lane = lax.broadcasted_iota(jnp.int32, (tm, LANES), 1)

    lf = jnp.where(lane < n_gate, _log_sigmoid(_dot(xn, wf_ref[...]) + bf_ref[...]), 0.0)
    lf_out[...] = lf[:, :n_gate]

    @pl.when(i % tiles_per_seq == 0)
    def _():
        carry_ref[...] = jnp.zeros_like(carry_ref)

    c = _dot_exact_rhs(tri_ref[...], lf) + carry_ref[...]
    carry_ref[...] = c[tm - 1:tm, :]
    c = c * LOG2E
    c_out[...] = c

    ch, cm, cl = _split3(c)
    cparts = (ch.astype(F32) + pltpu.roll(cm.astype(F32), 16, 1) + pltpu.roll(cl.astype(F32), 32, 1)).astype(BF)

    g1 = g1_ref[...]
    v_ones = jnp.where(lane >= hd, 1.0, 0.0)
    for j in range(n_heads // 2):
        sl = slice(2 * HEAD_PAD * j, 2 * HEAD_PAD * (j + 1))
        zq = _dot(xn, wq_ref[:, sl])
        aug = _dot(cparts, pq_ref[:, sl]) + oq_ref[:, sl]
        for u in range(2):
            z = zq[:, HEAD_PAD * u:HEAD_PAD * (u + 1)]
            ms = _dot((z * z).astype(BF), g1)
            qn = z * lax.rsqrt(ms + EPS) * gq_ref[...]
            q_out[2 * j + u] = (qn + aug[:, HEAD_PAD * u:HEAD_PAD * (u + 1)]).astype(BF)

    for j in range(n_kv // 2):
        sl = slice(2 * HEAD_PAD * j, 2 * HEAD_PAD * (j + 1))
        zk = _dot(xn, wk_ref[:, sl])
        zv = _dot(xn, wv_ref[:, sl])
        aug = _dot(cparts, pk_ref[:, sl]) + ok_ref[:, sl]
        for u in range(2):
            z = zk[:, HEAD_PAD * u:HEAD_PAD * (u + 1)]
            ms = _dot((z * z).astype(BF), g1)
            kn = z * lax.rsqrt(ms + EPS) * gk_ref[...]
            ka_out[2 * j + u] = (kn + aug[:, HEAD_PAD * u:HEAD_PAD * (u + 1)]).astype(BF)
            vp_out[2 * j + u] = (zv[:, HEAD_PAD * u:HEAD_PAD * (u + 1)] + v_ones).astype(BF)

    zkv = _dot(xn, wkv_ref[...])
    nk = wkv_ref.shape[1] // 2
    k = zkv[:, :nk]
    ms = _dot((k * k).astype(BF), g2_ref[...])
    k_out[...] = k * lax.rsqrt(ms + EPS) * gk2_ref[...]
    v_out[...] = zkv[:, nk:]


def _fox_in(x, w, *, tm, tiles_per_seq, tri):
    n, d = x.shape
    n_heads, n_kv, n_gate, hd = w["n_heads"], w["n_kv"], w["n_gate"], w["head_dim"]
    kern = functools.partial(_fox_in_kernel, tiles_per_seq=tiles_per_seq, n_heads=n_heads, n_kv=n_kv, n_gate=n_gate,
                             hd=hd)
    consts = [w["ga"], w["wq"], w["wk"], w["wv"], w["wkv"], w["wf"], w["bf"], w["gq"], w["gk"], w["gk2"],
              w["g1"], w["g2"], tri, w["pq"], w["pk"], w["oq"], w["ok"]]
    return pl.pallas_call(
        kern,
        grid=(n // tm,),
        in_specs=[pl.BlockSpec((tm, d), lambda i: (i, 0))] + [_full(c.shape) for c in consts],
        out_specs=[
            pl.BlockSpec((n_heads, tm, HEAD_PAD), lambda i: (0, i, 0)),
            pl.BlockSpec((n_kv, tm, HEAD_PAD), lambda i: (0, i, 0)),
            pl.BlockSpec((n_kv, tm, HEAD_PAD), lambda i: (0, i, 0)),
            pl.BlockSpec((tm, n_kv * hd), lambda i: (i, 0)),
            pl.BlockSpec((tm, n_kv * hd), lambda i: (i, 0)),
            pl.BlockSpec((tm, n_gate), lambda i: (i, 0)),
            pl.BlockSpec((tm, LANES), lambda i: (i, 0)),
        ],
        out_shape=[
            jax.ShapeDtypeStruct((n_heads, n, HEAD_PAD), BF),
            jax.ShapeDtypeStruct((n_kv, n, HEAD_PAD), BF),
            jax.ShapeDtypeStruct((n_kv, n, HEAD_PAD), BF),
            jax.ShapeDtypeStruct((n, n_kv * hd), F32),
            jax.ShapeDtypeStruct((n, n_kv * hd), F32),
            jax.ShapeDtypeStruct((n, n_gate), F32),
            jax.ShapeDtypeStruct((n, LANES), F32),
        ],
        scratch_shapes=[pltpu.VMEM((1, LANES), F32)],
        compiler_params=_params(("arbitrary",)),
        name="fox_in",
    )(x, *consts)


def _fox_weights(w_in, b_f, g_q, g_k, g_attn, n_heads, n_kv, hd):
    d = w_in.shape[0]
    nq, nk = n_heads * hd, n_kv * hd
    n_gate = n_heads
    grp = n_heads // n_kv
    scale = hd ** -0.5
    pad = HEAD_PAD - hd

    def heads_padded(wm, nh):
        return jnp.pad(wm.reshape(d, nh, hd), ((0, 0), (0, 0), (0, pad))).reshape(d, nh * HEAD_PAD).astype(BF)

    ck0, cq0 = hd, hd + 3 * grp
    pq = np.zeros((LANES, n_heads * HEAD_PAD), np.float32)
    oq = np.zeros((1, n_heads * HEAD_PAD), np.float32)
    pk = np.zeros((LANES, n_kv * HEAD_PAD), np.float32)
    ok = np.zeros((1, n_kv * HEAD_PAD), np.float32)
    for h in range(n_heads):
        kv, g = divmod(h, grp)
        for p in range(3):
            pq[16 * p + h, HEAD_PAD * h + cq0 + p] = 1.0
            oq[0, HEAD_PAD * h + ck0 + 3 * g + p] = 1.0
            pk[16 * p + h, HEAD_PAD * kv + ck0 + 3 * g + p] = -1.0
    for kv in range(n_kv):
        for p in range(3):
            ok[0, HEAD_PAD * kv + cq0 + p] = 1.0
    g1 = np.zeros((HEAD_PAD, HEAD_PAD), np.float32)
    g1[:hd, :hd] = 1.0 / hd
    g2 = np.kron(np.eye(n_kv, dtype=np.float32), np.full((hd, hd), 1.0 / hd, np.float32))
    return dict(
        n_heads=n_heads, n_kv=n_kv, n_gate=n_gate, head_dim=hd,
        ga=g_attn.reshape(1, d),
        wq=heads_padded(w_in[:, :nq], n_heads),
        wk=heads_padded(w_in[:, nq:nq + nk], n_kv),
        wv=heads_padded(w_in[:, nq + nk:nq + 2 * nk], n_kv),
        wkv=w_in[:, nq:nq + 2 * nk].astype(BF),
        wf=jnp.pad(w_in[:, nq + 2 * nk:], ((0, 0), (0, LANES - n_gate))).astype(BF),
        bf=jnp.pad(b_f, (0, LANES - n_gate)).reshape(1, LANES),
        gq=jnp.pad(g_q * (scale * LOG2E), (0, pad)).reshape(1, HEAD_PAD),
        gk=jnp.pad(g_k, (0, pad)).reshape(1, HEAD_PAD),
        gk2=jnp.tile(g_k, n_kv).reshape(1, nk),
        g1=jnp.asarray(g1, BF), g2=jnp.asarray(g2, BF),
        pq=jnp.asarray(pq, BF), pk=jnp.asarray(pk, BF), oq=jnp.asarray(oq), ok=jnp.asarray(ok),
    )


def _rope_lanes(x, cos, sin, lane, lo, half):
    nl = x.shape[-1]
    fwd = pltpu.roll(x, nl - half, 1)
    bwd = pltpu.roll(x, half, 1)
    first = (lane >= lo) & (lane < lo + half)
    second = (lane >= lo + half) & (lane < lo + 2 * half)
    rot = jnp.where(first, -fwd, jnp.where(second, bwd, 0.0))
    return jnp.where(first | second, x * cos + rot * sin, x)


def _mla_in_kernel(x_ref, cos_ref, sin_ref, ga_ref, win_ref, gcq_ref, gckv_ref, gkr_ref, wuq_ref, gq_ref,
                   wuk_ref, gk_ref, wuv_ref, g1_ref,
                   q_out, kk_out, vv_out, ckv_out, kpe_out,
                   *, n_heads, q_lora, kv_lora, nope, rope, vdim):
    tm = x_ref.shape[0]
    lane = lax.broadcasted_iota(jnp.int32, (tm, LANES), 1)
    cos, sin = cos_ref[...], sin_ref[...]
    xn = _rms_rows(x_ref[...], ga_ref[...]).astype(BF)
    z = _dot(xn, win_ref[...])
    cq = _rms_rows(z[:, :q_lora], gcq_ref[...]).astype(BF)
    ckv = _rms_rows(z[:, q_lora:q_lora + kv_lora], gckv_ref[...])
    ckv_out[...] = ckv
    ckv_b = ckv.astype(BF)
    kr = z[:, q_lora + kv_lora:]
    kr = kr * lax.rsqrt(jnp.sum(kr * kr, axis=-1, keepdims=True) * (1.0 / rope) + EPS) * gkr_ref[...]
    kpe = _rope_lanes(kr, cos, sin, lane, 0, rope // 2)
    kpe_out[...] = kpe[:, :rope]
    kpe_sh = pltpu.roll(kpe, nope, 1)

    g1 = g1_ref[...]
    v_ones = jnp.where(lane >= vdim, 1.0, 0.0)
    for j in range(n_heads // 2):
        sl = slice(2 * HEAD_PAD * j, 2 * HEAD_PAD * (j + 1))
        zq = _dot(cq, wuq_ref[:, sl])
        zk = _dot(ckv_b, wuk_ref[:, sl])
        zv = _dot(ckv_b, wuv_ref[:, sl])
        for u in range(2):
            bl = slice(HEAD_PAD * u, HEAD_PAD * (u + 1))
            q = zq[:, bl]
            q = q * lax.rsqrt(_dot((q * q).astype(BF), g1) + EPS) * gq_ref[...]
            q_out[2 * j + u] = _rope_lanes(q, cos, sin, lane, nope, rope // 2).astype(BF)
            k = zk[:, bl]
            k = k * lax.rsqrt(_dot((k * k).astype(BF), g1) + EPS) * gk_ref[...]
            kk_out[2 * j + u] = (k + kpe_sh).astype(BF)
            vv_out[2 * j + u] = (zv[:, bl] + v_ones).astype(BF)


def _mla_in(x, cos, sin, w, *, tm):
    n, d = x.shape
    n_heads, kv_lora, rope = w["n_heads"], w["kv_lora"], w["rope"]
    tbl_tiles = cos.shape[0] // tm
    kern = functools.partial(_mla_in_kernel, n_heads=n_heads, q_lora=w["q_lora"], kv_lora=kv_lora,
                             nope=w["nope"], rope=rope, vdim=w["vdim"])
    consts = [w["ga"], w["win"], w["gcq"], w["gckv"], w["gkr"], w["wuq"], w["gq"], w["wuk"], w["gk"], w["wuv"], w["g1"]]
    head_spec = pl.BlockSpec((n_heads, tm, HEAD_PAD), lambda i: (0, i, 0))
    head_shape = jax.ShapeDtypeStruct((n_heads, n, HEAD_PAD), BF)
    return pl.pallas_call(
        kern,
        grid=(n // tm,),
        in_specs=[pl.BlockSpec((tm, d), lambda i: (i, 0)),
                  pl.BlockSpec((tm, LANES), lambda i: (i % tbl_tiles, 0)),
                  pl.BlockSpec((tm, LANES), lambda i: (i % tbl_tiles, 0))] + [_full(c.shape) for c in consts],
        out_specs=[head_spec, head_spec, head_spec,
                   pl.BlockSpec((tm, kv_lora), lambda i: (i, 0)),
                   pl.BlockSpec((tm, rope), lambda i: (i, 0))],
        out_shape=[head_shape, head_shape, head_shape,
                   jax.ShapeDtypeStruct((n, kv_lora), F32),
                   jax.ShapeDtypeStruct((n, rope), F32)],
        compiler_params=_params(("parallel",)),
        name="mla_in",
    )(x, cos, sin, *consts)


def _mla_weights(w_in, g_cq, w_uq, g_qn, g_qr, g_ckv, g_kr, w_ukv, g_kn, g_attn):
    d = w_in.shape[0]
    q_lora, kv_lora, rope, nope = g_cq.shape[0], g_ckv.shape[0], g_kr.shape[0], g_qn.shape[0]
    n_heads = w_uq.shape[1]
    vdim = w_ukv.shape[2] - nope
    scale = (nope + rope) ** -0.5
    in_pad = -w_in.shape[1] % LANES
    w_uk, w_uv = w_ukv[..., :nope], w_ukv[..., nope:]

    def heads_padded(wm):
        r, nh, hd = wm.shape
        return jnp.pad(wm, ((0, 0), (0, 0), (0, HEAD_PAD - hd))).reshape(r, nh * HEAD_PAD).astype(BF)

    g1 = np.zeros((HEAD_PAD, HEAD_PAD), np.float32)
    g1[:nope, :nope] = 1.0 / nope
    g1[nope:nope + rope, nope:nope + rope] = 1.0 / rope
    wuk_dh = jnp.transpose(w_uk, (0, 2, 1)).reshape(kv_lora, nope * n_heads).astype(BF)
    wabs = jnp.transpose(w_uk * g_kn[None, None, :], (1, 2, 0))
    wabs = jnp.pad(wabs, ((0, 0), (0, HEAD_PAD - nope), (0, 0))).reshape(n_heads * HEAD_PAD, kv_lora).astype(BF)
    return dict(
        n_heads=n_heads, q_lora=q_lora, kv_lora=kv_lora, rope=rope, nope=nope, vdim=vdim,
        ga=g_attn.reshape(1, d),
        win=jnp.pad(w_in, ((0, 0), (0, in_pad))).astype(BF),
        gcq=g_cq.reshape(1, -1), gckv=g_ckv.reshape(1, -1),
        gkr=jnp.pad(g_kr, (0, LANES - rope)).reshape(1, LANES),
        wuq=heads_padded(w_uq),
        gq=jnp.pad(jnp.concatenate([g_qn, g_qr]) * (scale * LOG2E), (0, HEAD_PAD - nope - rope)).reshape(1, HEAD_PAD),
        wuk=heads_padded(w_uk), gk=jnp.pad(g_kn, (0, HEAD_PAD - nope)).reshape(1, HEAD_PAD),
        wuv=heads_padded(w_uv), g1=jnp.asarray(g1, BF),
        wuk_dh=wuk_dh, wabs=wabs, wuv_flat=w_uv.reshape(kv_lora, n_heads * vdim).astype(BF),
    )


def _rope_tables(pos, half):
    inv = ROPE_THETA ** (-jnp.arange(half, dtype=F32) / half)
    ang = pos.astype(F32)[:, None] * inv[None, :]
    reps = LANES // half
    return jnp.tile(jnp.cos(ang), (1, reps)), jnp.tile(jnp.sin(ang), (1, reps))


def _flash_kernel(q_ref, k_ref, v_ref, o_ref, s_ref, m_ref, acc_ref, *, n_q, n_k, vdim, tkc):
    qi = pl.program_id(2)
    tq = q_ref.shape[1]
    diag = (qi * tq) // tkc
    off = qi * tq - diag * tkc
    lane_tiles = tkc // LANES

    m_ref[...] = jnp.full_like(m_ref, NEG)
    acc_ref[...] = jnp.zeros_like(acc_ref)

    def lane_max(s):
        m = s[:, :LANES]
        for t in range(1, lane_tiles):
            m = jnp.maximum(m, s[:, LANES * t:LANES * (t + 1)])
        return m

    def score_chunk(kc, keep):
        k0 = pl.multiple_of(kc * tkc, tkc)
        for r in range(n_q):
            s = _dot_nt(q_ref[r], k_ref[r * n_k // n_q, pl.ds(k0, tkc), :])
            if keep is not None:
                s = jnp.where(keep, s, NEG)
            s_ref[r, kc] = s
            m_ref[r] = jnp.maximum(m_ref[r], lane_max(s))

    def pass_a(kc, carry):
        score_chunk(kc, None)
        return carry

    lax.fori_loop(0, diag, pass_a, 0)
    row = lax.broadcasted_iota(jnp.int32, (tq, tkc), 0)
    col = lax.broadcasted_iota(jnp.int32, (tq, tkc), 1)
    score_chunk(diag, col <= row + off)

    for r in range(n_q):
        m_ref[r] = jnp.broadcast_to(jnp.max(m_ref[r], axis=-1, keepdims=True), (tq, LANES))

    def pass_b(kc, carry):
        k0 = pl.multiple_of(kc * tkc, tkc)
        for r in range(n_q):
            m = m_ref[r]
            p = jnp.exp2(s_ref[r, kc] - jnp.concatenate([m] * lane_tiles, axis=1)).astype(BF)
            acc_ref[r] += _dot(p, v_ref[r * n_k // n_q, pl.ds(k0, tkc), :])
        return carry

    lax.fori_loop(0, diag + 1, pass_b, 0)

    lane = lax.broadcasted_iota(jnp.int32, (tq, LANES), 1)

    def normalised(r):
        a = acc_ref[r]
        return jnp.where(lane < vdim, a / jnp.where(lane < vdim, pltpu.roll(a, vdim, 1), 1.0), 0.0)

    for j in range(n_q // 2):
        o_ref[:, LANES * j:LANES * (j + 1)] = (
            normalised(2 * j) + pltpu.roll(normalised(2 * j + 1), vdim, 1)).astype(o_ref.dtype)


def _flash_prompt(q, k, v, *, batch, n_q, n_k, tq, tkc, vdim):
    hq, n, _ = q.shape
    seq = n // batch
    nb = seq // tq
    assert tkc % tq == 0 and seq % tkc == 0 and 2 * vdim == LANES
    kern = functools.partial(_flash_kernel, n_q=n_q, n_k=n_k, vdim=vdim, tkc=tkc)
    return pl.pallas_call(
        kern,
        grid=(batch, hq // n_q, nb),
        in_specs=[
            pl.BlockSpec((n_q, tq, HEAD_PAD), lambda b, j, i: (j, b * nb + i, 0)),
            pl.BlockSpec((n_k, seq, HEAD_PAD), lambda b, j, i: (j, b, 0)),
            pl.BlockSpec((n_k, seq, HEAD_PAD), lambda b, j, i: (j, b, 0)),
        ],
        out_specs=pl.BlockSpec((tq, n_q * vdim), lambda b, j, i: (b * nb + i, j)),
        out_shape=jax.ShapeDtypeStruct((n, hq * vdim), BF),
        scratch_shapes=[pltpu.VMEM((n_q, seq // tkc, tq, tkc), F32), pltpu.VMEM((n_q, tq, LANES), F32),
                        pltpu.VMEM((n_q, tq, HEAD_PAD), F32)],
        compiler_params=_params(("parallel", "parallel", "arbitrary")),
        name="flash_prompt",
    )(q, k, v)


def _out_ffn_kernel(h_ref, o_ref, wo_ref, gf_ref, wg_ref, wu_ref, wd_ref, out_ref, h1_ref, xn_ref, acc_ref):
    f = pl.program_id(1)

    @pl.when(f == 0)
    def _():
        h1 = h_ref[...] + _dot(o_ref[...].astype(BF), wo_ref[...])
        h1_ref[...] = h1
        xn_ref[...] = _rms_rows(h1, gf_ref[...]).astype(BF)
        acc_ref[...] = jnp.zeros_like(acc_ref)

    xn = xn_ref[...]
    g = _dot(xn, wg_ref[...])
    u = _dot(xn, wu_ref[...])
    acc_ref[...] += _dot((g * jax.nn.sigmoid(g) * u).astype(BF), wd_ref[...])

    @pl.when(f == pl.num_programs(1) - 1)
    def _():
        out_ref[...] = h1_ref[...] + acc_ref[...]


def _out_ffn(h, o, wo, gf, wg, wu, wd, *, tm, tf):
    n, d = h.shape
    do = o.shape[1]
    dff = wg.shape[1]
    return pl.pallas_call(
        _out_ffn_kernel,
        grid=(n // tm, dff // tf),
        in_specs=[
            pl.BlockSpec((tm, d), lambda i, f: (i, 0)),
            pl.BlockSpec((tm, do), lambda i, f: (i, 0)),
            pl.BlockSpec((do, d), lambda i, f: (0, 0)),
            pl.BlockSpec((1, d), lambda i, f: (0, 0)),
            pl.BlockSpec((d, tf), lambda i, f: (0, f)),
            pl.BlockSpec((d, tf), lambda i, f: (0, f)),
            pl.BlockSpec((tf, d), lambda i, f: (f, 0)),
        ],
        out_specs=pl.BlockSpec((tm, d), lambda i, f: (i, 0)),
        out_shape=jax.ShapeDtypeStruct((n, d), F32),
        scratch_shapes=[pltpu.VMEM((tm, d), F32), pltpu.VMEM((tm, d), BF), pltpu.VMEM((tm, d), F32)],
        compiler_params=_params(("parallel", "arbitrary")),
        name="out_ffn",
    )(h, o, wo, gf, wg, wu, wd)


def _softmax_step(s, v, m_ref, l_ref, acc_ref, v_is_transposed=False):
    m_prev = m_ref[...]
    m_new = jnp.maximum(m_prev, jnp.max(s, axis=-1, keepdims=True))
    alpha = jnp.exp2(m_prev - m_new)
    p = jnp.exp2(s - m_new)
    l_ref[...] = alpha * l_ref[...] + jnp.sum(p, axis=-1, keepdims=True)
    p = p.astype(BF)
    acc_ref[...] = alpha * acc_ref[...] + (_dot_nt(p, v) if v_is_transposed else _dot(p, v))
    m_ref[...] = m_new


def _rows_from_heads(x, t_new):
    return jnp.concatenate([jnp.broadcast_to(x[h:h + 1], (t_new, x.shape[1])) for h in range(x.shape[0])], axis=0)


def _chunk_pipeline(pt_ref, srcs, bufs, sems, *, layer, n_pages, pps, nch, reverse, compute):
    b = pl.program_id(0)
    n_seq = pl.num_programs(0)

    def chunk_copies(seq, c, slot):
        first = ((nch - 1 - c) if reverse else c) * pps
        out = []
        for p in range(pps):
            pid = pt_ref[seq * n_pages + first + p]
            for i, (src, buf) in enumerate(zip(srcs, bufs)):
                out.append(pltpu.make_async_copy(src.at[layer, pid], buf.at[slot, p], sems.at[i, slot]))
        return out

    def start_all(copies):
        for n, cp in enumerate(copies):
            cp.start(priority=n % 2)

    @pl.when(b == 0)
    def _():
        start_all(chunk_copies(b, 0, 0))

    for c in range(nch):
        slot = c % 2
        if c + 1 < nch:
            start_all(chunk_copies(b, c + 1, 1 - slot))
        else:
            @pl.when(b + 1 < n_seq)
            def _():
                start_all(chunk_copies(b + 1, 0, 1 - slot))
        for cp in chunk_copies(b, c, slot):
            cp.wait()
        compute(c, slot)


def _fox_decode_kernel(pt_ref, q_ref, c_ref, k2_ref, v2_ref, u_ref, pg_ref, sg_ref, ck_hbm, cv_hbm, cl_hbm,
                       o_ref, kbuf, vbuf, lbuf, sems, qbd_ref, m_ref, l_ref, acc_ref, kc_ref, vc_ref,
                       *, layer, n_pages, pps, nch, t_new, n_heads, n_kv, hd):
    rows = n_heads * t_new
    page = kbuf.shape[3]
    kvw = n_kv * hd
    row = lax.broadcasted_iota(jnp.int32, (rows, LANES), 0)
    lane = lax.broadcasted_iota(jnp.int32, (rows, LANES), 1)

    a = q_ref[...].astype(F32).reshape(rows, HEAD_PAD)
    a = jnp.where(lane < hd, a, 0.0)
    kvh = row // (rows // n_kv)
    for blk in range(kvw // LANES):
        parts = 0.0
        for u in range(LANES // hd):
            parts = parts + jnp.where(kvh == blk * (LANES // hd) + u, pltpu.roll(a, hd * u, 1) if u else a, 0.0)
        qbd_ref[:, LANES * blk:LANES * (blk + 1)] = parts.astype(BF)
    cn = c_ref[...]
    cn_rows = jnp.concatenate([cn] * n_heads, axis=0)
    cn_row = jnp.sum(jnp.where(lane == row // t_new, cn_rows, 0.0), axis=-1, keepdims=True)
    m_ref[...] = jnp.full_like(m_ref, NEG)
    l_ref[...] = jnp.zeros_like(l_ref)
    acc_ref[...] = jnp.zeros_like(acc_ref)
    zpad = jnp.zeros((page - t_new, kvw), F32)
    kn = jnp.concatenate([k2_ref[...], zpad], axis=0).astype(BF)
    vn = jnp.concatenate([v2_ref[...], zpad], axis=0).astype(BF)
    cn_pad = jnp.concatenate([cn, jnp.zeros((page - t_new, LANES), F32)], axis=0)
    cn_t = cn_pad.T[:n_heads]
    bias = cn_row - _rows_from_heads(cn_t, t_new)
    keep = (lane < t_new) & (lane <= row % t_new)
    s = jnp.where(keep, _dot_nt(qbd_ref[...], kn) + bias, NEG)
    _softmax_step(s, vn, m_ref, l_ref, acc_ref)

    u_mat = u_ref[...]
    carry = [jnp.zeros((n_heads, 1), F32)]

    def compute(c, slot):
        lf_t = jnp.concatenate([lbuf[slot, p] for p in range(pps)], axis=0)
        within = _dot_exact_lhs(lf_t, u_mat)
        total = jnp.sum(lf_t, axis=-1, keepdims=True)
        sfx = [None] * pps
        for p in reversed(range(pps)):
            sfx[p] = within[n_heads * p:n_heads * (p + 1)] + carry[0]
            carry[0] = carry[0] + total[n_heads * p:n_heads * (p + 1)]
            kc_ref[:, page * p:page * (p + 1)] = kbuf[slot, p].astype(BF)
            vc_ref[:, page * p:page * (p + 1)] = vbuf[slot, p].astype(BF)
        bias = _rows_from_heads(jnp.concatenate(sfx, axis=1) * LOG2E, t_new) + cn_row
        s = _dot(qbd_ref[...], kc_ref[...]) + bias
        _softmax_step(s, vc_ref[...], m_ref, l_ref, acc_ref, v_is_transposed=True)

    _chunk_pipeline(pt_ref, (ck_hbm, cv_hbm, cl_hbm), (kbuf, vbuf, lbuf), sems, layer=layer, n_pages=n_pages,
                    pps=pps, nch=nch, reverse=True, compute=compute)

    o = acc_ref[...] / l_ref[...]
    row2 = lax.broadcasted_iota(jnp.int32, (rows, kvw), 0)
    lane2 = lax.broadcasted_iota(jnp.int32, (rows, kvw), 1)
    om = jnp.where(lane2 // hd == row2 // (rows // n_kv), o, 0.0).astype(BF)
    out = jnp.zeros(o_ref.shape, F32)
    for g in range(n_heads // n_kv):
        out = out + _dot(sg_ref[g], _dot(om, pg_ref[g]).astype(BF))
    o_ref[...] = out


def _fox_decode(pt, q2, c2, k2, v2, cache_k, cache_v, cache_lf, layer, *, t_new, pps):
    n_heads, ns, _ = q2.shape
    b2 = ns // t_new
    _, n_pool, kvw, page = cache_k.shape
    n_gate = cache_lf.shape[2]
    n_pages = pt.shape[0] // b2
    nch = n_pages // pps
    assert nch % 2 == 0
    hd = 64
    n_kv = kvw // hd
    grp = n_heads // n_kv
    rows = n_heads * t_new
    u_mat = jnp.asarray(np.tril(np.ones((page, page), np.float32), -1), BF)
    pg = np.zeros((grp, kvw, n_heads * hd), np.float32)
    sg = np.zeros((grp, t_new, rows), np.float32)
    for h in range(n_heads):
        kv, g = divmod(h, grp)
        for dd in range(hd):
            pg[g, kv * hd + dd, h * hd + dd] = 1.0
        for t in range(t_new):
            sg[g, t, h * t_new + t] = 1.0
    consts = [u_mat, jnp.asarray(pg, BF), jnp.asarray(sg, BF)]
    kern = functools.partial(_fox_decode_kernel, layer=layer, n_pages=n_pages, pps=pps, nch=nch, t_new=t_new,
                             n_heads=n_heads, n_kv=n_kv, hd=hd)
    grid_spec = pltpu.PrefetchScalarGridSpec(
        num_scalar_prefetch=1,
        grid=(b2,),
        in_specs=[
            pl.BlockSpec((n_heads, t_new, HEAD_PAD), lambda b, ptr: (0, b, 0)),
            pl.BlockSpec((t_new, LANES), lambda b, ptr: (b, 0)),
            pl.BlockSpec((t_new, kvw), lambda b, ptr: (b, 0)),
            pl.BlockSpec((t_new, kvw), lambda b, ptr: (b, 0)),
        ] + [pl.BlockSpec(x.shape, lambda b, ptr, nd=x.ndim: (0,) * nd) for x in consts]
        + [pl.BlockSpec(memory_space=pl.ANY)] * 3,
        out_specs=pl.BlockSpec((t_new, n_heads * hd), lambda b, ptr: (b, 0)),
        scratch_shapes=[
            pltpu.VMEM((2, pps, kvw, page), F32), pltpu.VMEM((2, pps, kvw, page), F32),
            pltpu.VMEM((2, pps, n_gate, page), F32), pltpu.SemaphoreType.DMA((3, 2)),
            pltpu.VMEM((rows, kvw), BF),
            pltpu.VMEM((rows, 1), F32), pltpu.VMEM((rows, 1), F32), pltpu.VMEM((rows, kvw), F32),
            pltpu.VMEM((kvw, pps * page), BF), pltpu.VMEM((kvw, pps * page), BF),
        ],
    )
    return pl.pallas_call(
        kern, grid_spec=grid_spec,
        out_shape=jax.ShapeDtypeStruct((ns, n_heads * hd), F32),
        compiler_params=_params(("arbitrary",)),
        name="fox_decode",
    )(pt, q2, c2, k2, v2, *consts, cache_k, cache_v, cache_lf)


def _mla_decode_kernel(pt_ref, q_ref, c2_ref, r2_ref, wdh_ref, rq_ref, wabs_ref, wuv_ref, sel_ref, cc_hbm, cr_hbm,
                       o_ref, cbuf, rbuf, sems, qa_ref, qr_ref, m_ref, l_ref, acc_ref, cc_ref, rc_ref, rn_ref,
                       *, layer, n_pages, pps, nch, t_new, n_heads, nope, rope, vdim, kblk):
    rows = n_heads * t_new
    page = cbuf.shape[2]
    row = lax.broadcasted_iota(jnp.int32, (rows, LANES), 0)
    lane = lax.broadcasted_iota(jnp.int32, (rows, LANES), 1)

    def nope_scores(n_keys):
        blk = min(kblk, n_keys)
        inv = []
        for kb in range(n_keys // blk):
            kf = _dot(cc_ref[kb * blk:(kb + 1) * blk, :], wdh_ref[...])
            sq = kf * kf
            part = sq[:, :LANES]
            for t in range(1, sq.shape[1] // LANES):
                part = part + sq[:, LANES * t:LANES * (t + 1)]
            ssq = _dot_nt(rq_ref[...], part.astype(BF))
            inv.append(lax.rsqrt(ssq * (1.0 / nope) + EPS))
        inv = inv[0] if len(inv) == 1 else jnp.concatenate(inv, axis=1)
        return _dot_nt(qa_ref[...], cc_ref[:n_keys, :]) * inv

    a = q_ref[...].astype(F32).reshape(rows, HEAD_PAD)
    an = jnp.where(lane < nope, a, 0.0)
    head = row // t_new
    qbd = jnp.concatenate([jnp.where(head == h, an, 0.0) for h in range(n_heads)], axis=1).astype(BF)
    qa_ref[...] = _dot(qbd, wabs_ref[...]).astype(BF)
    qr_ref[...] = jnp.where(lane < rope, pltpu.roll(a, HEAD_PAD - nope, 1), 0.0).astype(BF)
    m_ref[...] = jnp.full_like(m_ref, NEG)
    l_ref[...] = jnp.zeros_like(l_ref)
    acc_ref[...] = jnp.zeros_like(acc_ref)
    cc_ref[:page, :] = jnp.concatenate(
        [c2_ref[...], jnp.zeros((page - t_new, c2_ref.shape[1]), F32)], axis=0).astype(BF)
    rn_ref[...] = jnp.zeros_like(rn_ref)
    rn_ref[:t_new, :rope] = r2_ref[...].astype(BF)
    keep = (lane < t_new) & (lane <= row % t_new)
    s = jnp.where(keep, nope_scores(page) + _dot_nt(qr_ref[...], rn_ref[...]), NEG)
    _softmax_step(s, cc_ref[:page, :], m_ref, l_ref, acc_ref)

    def compute(c, slot):
        for p in range(pps):
            cc_ref[page * p:page * (p + 1), :] = cbuf[slot, p].astype(BF)
            rc_ref[:, page * p:page * (p + 1)] = rbuf[slot, p].astype(BF)
        s = nope_scores(pps * page) + _dot(qr_ref[:, :rope], rc_ref[...])
        _softmax_step(s, cc_ref[...], m_ref, l_ref, acc_ref)

    _chunk_pipeline(pt_ref, (cc_hbm, cr_hbm), (cbuf, rbuf), sems, layer=layer, n_pages=n_pages,
                    pps=pps, nch=nch, reverse=False, compute=compute)

    olat = (acc_ref[...] / l_ref[...]).astype(BF)
    of = _dot(olat, wuv_ref[...])
    row2 = lax.broadcasted_iota(jnp.int32, of.shape, 0)
    lane2 = lax.broadcasted_iota(jnp.int32, of.shape, 1)
    om = jnp.where(lane2 // vdim == row2 // t_new, of, 0.0).astype(BF)
    o_ref[...] = _dot(sel_ref[...], om)


def _mla_decode(pt, q2, ckv2, kpe2, cache_c, cache_r, layer, w, *, t_new, pps):
    n_heads, ns, _ = q2.shape
    b2 = ns // t_new
    _, n_pool, page, kv_lora = cache_c.shape
    rope, nope, vdim = w["rope"], w["nope"], w["vdim"]
    n_pages = pt.shape[0] // b2
    nch = n_pages // pps
    assert nch % 2 == 0
    rows = n_heads * t_new
    rq = np.zeros((rows, LANES), np.float32)
    sel = np.zeros((t_new, rows), np.float32)
    for r in range(rows):
        rq[r, np.arange(LANES) % n_heads == r // t_new] = 1.0
        sel[r % t_new, r] = 1.0
    consts = [w["wuk_dh"], jnp.asarray(rq, BF), w["wabs"], w["wuv_flat"], jnp.asarray(sel, BF)]
    kern = functools.partial(_mla_decode_kernel, layer=layer, n_pages=n_pages, pps=pps, nch=nch, t_new=t_new,
                             n_heads=n_heads, nope=nope, rope=rope, vdim=vdim, kblk=pps * page)
    grid_spec = pltpu.PrefetchScalarGridSpec(
        num_scalar_prefetch=1,
        grid=(b2,),
        in_specs=[
            pl.BlockSpec((n_heads, t_new, HEAD_PAD), lambda b, ptr: (0, b, 0)),
            pl.BlockSpec((t_new, kv_lora), lambda b, ptr: (b, 0)),
            pl.BlockSpec((t_new, rope), lambda b, ptr: (b, 0)),
        ] + [pl.BlockSpec(x.shape, lambda b, ptr, nd=x.ndim: (0,) * nd) for x in consts]
        + [pl.BlockSpec(memory_space=pl.ANY)] * 2,
        out_specs=pl.BlockSpec((t_new, n_heads * vdim), lambda b, ptr: (b, 0)),
        scratch_shapes=[
            pltpu.VMEM((2, pps, page, kv_lora), F32), pltpu.VMEM((2, pps, rope, page), F32),
            pltpu.SemaphoreType.DMA((2, 2)),
            pltpu.VMEM((rows, kv_lora), BF), pltpu.VMEM((rows, LANES), BF),
            pltpu.VMEM((rows, 1), F32), pltpu.VMEM((rows, 1), F32), pltpu.VMEM((rows, kv_lora), F32),
            pltpu.VMEM((pps * page, kv_lora), BF), pltpu.VMEM((rope, pps * page), BF),
            pltpu.VMEM((page, LANES), BF),
        ],
    )
    return pl.pallas_call(
        kern, grid_spec=grid_spec,
        out_shape=jax.ShapeDtypeStruct((ns, n_heads * vdim), F32),
        compiler_params=_params(("arbitrary",)),
        name="mla_decode",
    )(pt, q2, ckv2, kpe2, *consts, cache_c, cache_r)


TM = 512
TQ = 256
TQ_MLA = 512
TKC = 1024
PPS = 32


def _ffn_chunk(dff):
    for parts in (2, 1, 4, 11, 22):
        if dff % parts == 0 and (dff // parts) % LANES == 0:
            return dff // parts
    return dff


def kernel(x_prompt, x_sample, cache_fox_k, cache_fox_v, cache_fox_logf, cache_mla_ckv, cache_mla_kpe, page_table, attn_norm, ffn_norm, fox_w_in, fox_b_f, fox_g_q, fox_g_k, fox_w_o, mla_w_in, mla_g_cq, mla_w_uq, mla_g_qn, mla_g_qr, mla_g_ckv, mla_g_kr, mla_w_ukv, mla_g_kn, mla_w_o, ffn_w_gu, ffn_w_down):
    b1, seq, d = x_prompt.shape
    b2, t_new, _ = x_sample.shape
    depth = attn_norm.shape[0]
    n_fox, n_pool, page, fox_kv, fox_hd = cache_fox_k.shape
    fox_heads = fox_b_f.shape[1]
    past = page_table.shape[1] * page
    dff = ffn_w_down.shape[1]
    tf = _ffn_chunk(dff)
    n_p, n_s = b1 * seq, b2 * t_new
    tm_s = min(TM, n_s)
    assert fox_hd == 64 and seq % TKC == 0 and seq % TM == 0 and n_s % tm_s == 0 and tm_s % t_new == 0
    assert page_table.shape[1] % PPS == 0 and page == LANES and t_new == SUBLANES

    hp = x_prompt.reshape(n_p, d)
    hs = x_sample.reshape(n_s, d)
    pt = page_table.reshape(-1).astype(jnp.int32)
    ck4 = jnp.transpose(cache_fox_k, (0, 1, 3, 4, 2)).reshape(n_fox, n_pool, fox_kv * fox_hd, page)
    cv4 = jnp.transpose(cache_fox_v, (0, 1, 3, 4, 2)).reshape(n_fox, n_pool, fox_kv * fox_hd, page)
    clf = jnp.transpose(cache_fox_logf, (0, 1, 3, 2))
    ckr = jnp.transpose(cache_mla_kpe, (0, 1, 3, 2))

    tri_p = jnp.asarray(np.tril(np.ones((TM, TM), np.float32)), BF)
    seq_id = np.arange(tm_s) // t_new
    tri_s = jnp.asarray(np.tril(np.ones((tm_s, tm_s), np.float32)) * (seq_id[:, None] == seq_id[None, :]), BF)
    rope_half = mla_g_kr.shape[1] // 2
    cos_p, sin_p = _rope_tables(jnp.arange(seq), rope_half)
    cos_s, sin_s = _rope_tables(past + jnp.arange(tm_s) % t_new, rope_half)

    fk_p, fv_p, fl_p, fk_s, fv_s, fl_s = [], [], [], [], [], []
    mc_p, mr_p, mc_s, mr_s = [], [], [], []
    for i in range(depth):
        j = i // 2
        if i % 2 == 0:
            w = _fox_weights(fox_w_in[j], fox_b_f[j], fox_g_q[j], fox_g_k[j], attn_norm[i], fox_heads, fox_kv, fox_hd)
            q, ka, vp, k, v, lf, _ = _fox_in(hp, w, tm=TM, tiles_per_seq=seq // TM, tri=tri_p)
            o_p = _flash_prompt(q, ka, vp, batch=b1, n_q=fox_heads // fox_kv, n_k=1, tq=TQ, tkc=TKC, vdim=fox_hd)
            q2, _, _, k2, v2, lf2, c2 = _fox_in(hs, w, tm=tm_s, tiles_per_seq=1, tri=tri_s)
            o_s = _fox_decode(pt, q2, c2, k2, v2, ck4, cv4, clf, j, t_new=t_new, pps=PPS)
            fk_p.append(k); fv_p.append(v); fl_p.append(lf)
            fk_s.append(k2); fv_s.append(v2); fl_s.append(lf2)
            wo = fox_w_o[j].astype(BF)
        else:
            w = _mla_weights(mla_w_in[j], mla_g_cq[j], mla_w_uq[j], mla_g_qn[j], mla_g_qr[j], mla_g_ckv[j],
                             mla_g_kr[j], mla_w_ukv[j], mla_g_kn[j], attn_norm[i])
            q, kk, vv, ckv, kpe = _mla_in(hp, cos_p, sin_p, w, tm=TM)
            o_p = _flash_prompt(q, kk, vv, batch=b1, n_q=2, n_k=2, tq=TQ_MLA, tkc=TKC, vdim=w["vdim"])
            q2, _, _, ckv2, kpe2 = _mla_in(hs, cos_s, sin_s, w, tm=tm_s)
            o_s = _mla_decode(pt, q2, ckv2, kpe2, cache_mla_ckv, ckr, j, w, t_new=t_new, pps=PPS)
            mc_p.append(ckv); mr_p.append(kpe)
            mc_s.append(ckv2); mr_s.append(kpe2)
            wo = mla_w_o[j].astype(BF)
        gf = ffn_norm[i].reshape(1, d)
        wg = ffn_w_gu[i, :, :dff].astype(BF)
        wu = ffn_w_gu[i, :, dff:].astype(BF)
        wd = ffn_w_down[i].astype(BF)
        hp = _out_ffn(hp, o_p, wo, gf, wg, wu, wd, tm=TM, tf=tf)
        hs = _out_ffn(hs, o_s, wo, gf, wg, wu, wd, tm=tm_s, tf=tf)

    def stk(xs, lead, tail):
        return jnp.stack(xs).reshape((len(xs),) + lead + tail)

    return (hp.reshape(b1, seq, d), hs.reshape(b2, t_new, d),
            stk(fk_p, (b1, seq), (fox_kv, fox_hd)), stk(fv_p, (b1, seq), (fox_kv, fox_hd)),
            stk(fl_p, (b1, seq), (fox_heads,)),
            stk(mc_p, (b1, seq), (mla_g_ckv.shape[1],)), stk(mr_p, (b1, seq), (mla_g_kr.shape[1],)),
            stk(fk_s, (b2, t_new), (fox_kv, fox_hd)), stk(fv_s, (b2, t_new), (fox_kv, fox_hd)),
            stk(fl_s, (b2, t_new), (fox_heads,)),
            stk(mc_s, (b2, t_new), (mla_g_ckv.shape[1],)), stk(mr_s, (b2, t_new), (mla_g_kr.shape[1],)))
```
